```python
import jax
import jax.numpy as jnp
from jax import lax
import numpy as np

D_MODEL = 4096
BATCH = 4
SEQ = 2048
DEPTH = 2

RET_HEAD_DIM = 256
RET_WIDTH = D_MODEL // 2
RET_HEADS = RET_WIDTH // RET_HEAD_DIM
RET_CHUNK = 128
FOX_HEAD_DIM = 128
FOX_WIDTH = D_MODEL // 2
FOX_HEADS = FOX_WIDTH // FOX_HEAD_DIM
FOX_BLOCK = 128
LRU_WIDTH = D_MODEL // 2
LRU_BLOCKS = 16
LRU_BLOCK_DIM = LRU_WIDTH // LRU_BLOCKS
LRU_CONV_WIDTH = 4
LRU_C = 8.0
N_BRANCH = 3
D_FF = 2 * D_MODEL
HALF_STEP = 0.5
ROPE_BASE = 10000.0
EPS = 1e-6
IN_SIZES = (RET_WIDTH, RET_WIDTH, RET_WIDTH, RET_WIDTH,
            FOX_WIDTH, FOX_WIDTH, FOX_WIDTH, FOX_HEADS,
            LRU_WIDTH, LRU_WIDTH,
            N_BRANCH * D_MODEL)
N_IN = sum(IN_SIZES)

kernel_name = 'hybrid_retention_fox_rglru_macaron'


def rms_norm(x, g):
    xf = x.astype(jnp.float32)
    y = xf * lax.rsqrt(jnp.mean(xf * xf, axis=-1, keepdims=True) + EPS)
    return (y * g.astype(jnp.float32)).astype(x.dtype)


def swiglu(x, w_gate, w_up, w_down):
    return (jax.nn.silu(x @ w_gate) * (x @ w_up)) @ w_down


def apply_rotary(x, pos):
    half = x.shape[-1] // 2
    inv_freq = ROPE_BASE ** (-jnp.arange(half, dtype=jnp.float32) / half)
    ang = pos.astype(jnp.float32)[:, None] * inv_freq[None, :]
    cos = jnp.cos(ang)[None, :, None, :]
    sin = jnp.sin(ang)[None, :, None, :]
    xf = x.astype(jnp.float32)
    x1, x2 = xf[..., :half], xf[..., half:]
    return jnp.concatenate([x1 * cos - x2 * sin, x2 * cos + x1 * sin], axis=-1).astype(x.dtype)


def retention_chunkwise(q, k, v):
    B, S, H, d = q.shape
    C = RET_CHUNK
    n = S // C
    f32 = jnp.float32
    log_gamma = jnp.log(1.0 - jnp.exp2(-5.0 - jnp.arange(H, dtype=f32)))
    qc = q.astype(f32).reshape(B, n, C, H, d)
    kc = (k.astype(f32) * d ** -0.5).reshape(B, n, C, H, d)
    vc = v.astype(f32).reshape(B, n, C, H, d)
    idx = jnp.arange(C, dtype=f32)
    rel = idx[:, None] - idx[None, :]
    inner_decay = jnp.where(rel[None] >= 0,
                            jnp.exp(jnp.maximum(rel, 0.0)[None] * log_gamma[:, None, None]),
                            0.0)
    scores = jnp.einsum('bnihd,bnjhd->bnhij', qc, kc) * inner_decay
    inner = jnp.einsum('bnhij,bnjhe->bnihe', scores, vc)
    k_to_end = jnp.exp((C - 1.0 - idx)[:, None] * log_gamma[None, :])
    kv = jnp.einsum('bnjhd,bnjhe->bnhde', kc * k_to_end[:, :, None], vc)
    chunk_decay = jnp.exp(C * log_gamma)[:, None, None]

    def step(state, kv_n):
        return chunk_decay * state + kv_n, state

    _, prev = lax.scan(step, jnp.zeros((B, H, d, d), f32), jnp.moveaxis(kv, 1, 0))
    prev = jnp.moveaxis(prev, 0, 1)
    q_from_start = jnp.exp((idx + 1.0)[:, None] * log_gamma[None, :])
    cross = jnp.einsum('bnihd,bnhde->bnihe', qc * q_from_start[:, :, None], prev)
    return (inner + cross).reshape(B, S, H, d).astype(q.dtype)


def forgetting_attention(q, k, v, log_f):
    B, S, H, d = q.shape
    cum = jnp.cumsum(log_f, axis=1).transpose(0, 2, 1)
    scale = d ** -0.5
    outs = []
    for i in range(S // FOX_BLOCK):
        lo, hi = i * FOX_BLOCK, (i + 1) * FOX_BLOCK
        s = jnp.einsum('bqhd,bkhd->bhqk', q[:, lo:hi], k[:, :hi]).astype(jnp.float32) * scale
        bias = cum[:, :, lo:hi, None] - cum[:, :, None, :hi]
        causal = (lo + jnp.arange(FOX_BLOCK))[:, None] >= jnp.arange(hi)[None, :]
        p = jax.nn.softmax(jnp.where(causal, s + bias, -jnp.inf), axis=-1)
        outs.append(jnp.einsum('bhqk,bkhd->bqhd', p.astype(v.dtype), v[:, :hi]))
    return jnp.concatenate(outs, axis=1)


def causal_depthwise_conv(x, w, b):
    K = w.shape[0]
    S = x.shape[1]
    xp = jnp.pad(x, ((0, 0), (K - 1, 0), (0, 0)))
    return b + sum(xp[:, j:j + S] * w[j] for j in range(K))


def rg_lru(x, w_a, b_a, w_x, b_x, lam):
    B, S, W = x.shape
    xb = x.reshape(B, S, LRU_BLOCKS, LRU_BLOCK_DIM)
    r = jax.nn.sigmoid(jnp.einsum('bshi,hij->bshj', xb, w_a).reshape(B, S, W) + b_a)
    i = jax.nn.sigmoid(jnp.einsum('bshi,hij->bshj', xb, w_x).reshape(B, S, W) + b_x)
    log_a = -LRU_C * r.astype(jnp.float32) * jax.nn.softplus(-lam.astype(jnp.float32))
    a = jnp.exp(log_a)
    u = jnp.sqrt(-jnp.expm1(2.0 * log_a)) * (i.astype(jnp.float32) * x.astype(jnp.float32))

    def combine(left, right):
        a1, b1 = left
        a2, b2 = right
        return a1 * a2, a2 * b1 + b2

    _, h = lax.associative_scan(combine, (a, u), axis=1)
    return h.astype(x.dtype)


def setup_inputs(seed: int = 0) -> dict:
    key = jax.random.key(seed)
    ks = jax.random.split(key, 32)
    f32 = jnp.float32

    def dense(k, shape, fan_in):
        return jax.random.normal(k, shape, f32) * fan_in ** -0.5

    def gain(k, shape):
        return 1.0 + 0.02 * jax.random.normal(k, shape, f32)

    def small(k, shape, scale=0.01):
        return scale * jax.random.normal(k, shape, f32)

    u = jax.random.uniform(ks[17], (DEPTH, LRU_WIDTH), f32, 0.9, 0.999)
    s = u ** (1.0 / LRU_C)
    lru_lambda = jnp.log(s) - jnp.log1p(-s)
    return {
        'x': jax.random.normal(ks[0], (BATCH, SEQ, D_MODEL), f32),
        'ffn1_norm': gain(ks[1], (DEPTH, D_MODEL)),
        'ffn1_w_gate': dense(ks[2], (DEPTH, D_MODEL, D_FF), D_MODEL),
        'ffn1_w_up': dense(ks[3], (DEPTH, D_MODEL, D_FF), D_MODEL),
        'ffn1_w_down': dense(ks[4], (DEPTH, D_FF, D_MODEL), D_FF),
        'mix_norm': gain(ks[5], (DEPTH, D_MODEL)),
        'w_in': dense(ks[6], (DEPTH, D_MODEL, N_IN), D_MODEL),
        'ret_norm': gain(ks[7], (DEPTH, RET_HEADS, RET_HEAD_DIM)),
        'fox_q_norm': gain(ks[8], (DEPTH, FOX_HEAD_DIM)),
        'fox_k_norm': gain(ks[9], (DEPTH, FOX_HEAD_DIM)),
        'fox_f_bias': 2.0 + small(ks[10], (DEPTH, FOX_HEADS), 0.1),
        'lru_conv_w': dense(ks[11], (DEPTH, LRU_CONV_WIDTH, LRU_WIDTH), LRU_CONV_WIDTH),
        'lru_conv_b': small(ks[12], (DEPTH, LRU_WIDTH)),
        'lru_w_a': dense(ks[13], (DEPTH, LRU_BLOCKS, LRU_BLOCK_DIM, LRU_BLOCK_DIM), LRU_BLOCK_DIM),
        'lru_b_a': small(ks[14], (DEPTH, LRU_WIDTH)),
        'lru_w_x': dense(ks[15], (DEPTH, LRU_BLOCKS, LRU_BLOCK_DIM, LRU_BLOCK_DIM), LRU_BLOCK_DIM),
        'lru_b_x': small(ks[16], (DEPTH, LRU_WIDTH)),
        'lru_lambda': lru_lambda,
        'w_branch_ret': dense(ks[18], (DEPTH, RET_WIDTH, D_MODEL), RET_WIDTH),
        'w_branch_fox': dense(ks[19], (DEPTH, FOX_WIDTH, D_MODEL), FOX_WIDTH),
        'w_branch_lru': dense(ks[20], (DEPTH, LRU_WIDTH, D_MODEL), LRU_WIDTH),
        'w_out': dense(ks[21], (DEPTH, D_MODEL, D_MODEL), D_MODEL),
        'ffn2_norm': gain(ks[22], (DEPTH, D_MODEL)),
        'ffn2_w_gate': dense(ks[23], (DEPTH, D_MODEL, D_FF), D_MODEL),
        'ffn2_w_up': dense(ks[24], (DEPTH, D_MODEL, D_FF), D_MODEL),
        'ffn2_w_down': dense(ks[25], (DEPTH, D_FF, D_MODEL), D_FF),
    }


def reference(x, ffn1_norm, ffn1_w_gate, ffn1_w_up, ffn1_w_down, mix_norm, w_in, ret_norm,
              fox_q_norm, fox_k_norm, fox_f_bias, lru_conv_w, lru_conv_b, lru_w_a, lru_b_a,
              lru_w_x, lru_b_x, lru_lambda, w_branch_ret, w_branch_fox, w_branch_lru, w_out,
              ffn2_norm, ffn2_w_gate, ffn2_w_up, ffn2_w_down):
    B, S, _ = x.shape
    pos = jnp.arange(S)
    split_points = np.cumsum(IN_SIZES)[:-1].tolist()
    for l in range(DEPTH):
        x = x + HALF_STEP * swiglu(rms_norm(x, ffn1_norm[l]), ffn1_w_gate[l], ffn1_w_up[l], ffn1_w_down[l])

        h = rms_norm(x, mix_norm[l])
        proj = h @ w_in[l]
        (rq, rk, rv, rg, fq, fk, fv, fgate, lgate, lx, merge) = jnp.split(proj, split_points, axis=-1)

        rq = apply_rotary(rq.reshape(B, S, RET_HEADS, RET_HEAD_DIM), pos)
        rk = apply_rotary(rk.reshape(B, S, RET_HEADS, RET_HEAD_DIM), pos)
        ret = retention_chunkwise(rq, rk, rv.reshape(B, S, RET_HEADS, RET_HEAD_DIM))
        ret = jax.nn.silu(rg) * rms_norm(ret, ret_norm[l]).reshape(B, S, RET_WIDTH)

        fq = rms_norm(fq.reshape(B, S, FOX_HEADS, FOX_HEAD_DIM), fox_q_norm[l])
        fk = rms_norm(fk.reshape(B, S, FOX_HEADS, FOX_HEAD_DIM), fox_k_norm[l])
        log_f = jax.nn.log_sigmoid((fgate + fox_f_bias[l]).astype(jnp.float32))
        fox = forgetting_attention(fq, fk, fv.reshape(B, S, FOX_HEADS, FOX_HEAD_DIM), log_f)
        fox = fox.reshape(B, S, FOX_WIDTH)

        lx = causal_depthwise_conv(lx, lru_conv_w[l], lru_conv_b[l])
        lru = jax.nn.gelu(lgate) * rg_lru(lx, lru_w_a[l], lru_b_a[l], lru_w_x[l], lru_b_x[l], lru_lambda[l])

        g_ret, g_fox, g_lru = jnp.split(jax.nn.sigmoid(merge), N_BRANCH, axis=-1)
        mixed = (g_ret * (ret @ w_branch_ret[l])
                 + g_fox * (fox @ w_branch_fox[l])
                 + g_lru * (lru @ w_branch_lru[l]))
        x = x + mixed @ w_out[l]

        x = x + HALF_STEP * swiglu(rms_norm(x, ffn2_norm[l]), ffn2_w_gate[l], ffn2_w_up[l], ffn2_w_down[l])
    return x
```

```python
import functools
import math

import jax
import jax.numpy as jnp
from jax import lax
from jax.experimental import pallas as pl
from jax.experimental.pallas import tpu as pltpu

EPS = 1e-6
HALF_STEP = 0.5
ROPE_BASE = 10000.0
LRU_C = 8.0
N_BRANCH = 3

V7X_LANES = 128
V7X_VMEM_REQUEST_CAP = 60000 * 1024
ACT_DTYPE = jnp.bfloat16


def _vmem_limit(*block_bytes, scratch_bytes=0):
    need = 3 * sum(block_bytes) + scratch_bytes
    return int(min(max(need, 16 * 1024 * 1024), V7X_VMEM_REQUEST_CAP))


def _nbytes(shape, dtype):
    return math.prod(shape) * jnp.dtype(dtype).itemsize


def _tile(n, pref):
    t = min(n, pref)
    assert n % t == 0, (n, pref)
    return t


def _dot(a, b, dims=(((1,), (0,)), ((), ()))):
    return lax.dot_general(a, b, dims, preferred_element_type=jnp.float32)


_NT = (((1,), (1,)), ((), ()))
_TN = (((0,), (0,)), ((), ()))


def _sigmoid(x):
    return 1.0 / (1.0 + jnp.exp(-x))


def _softplus(x):
    return jnp.maximum(x, 0.0) + jnp.log1p(jnp.exp(-jnp.abs(x)))


def _rmsnorm_kernel(x_ref, g_ref, o_ref):
    x = x_ref[...]
    ms = jnp.mean(x * x, axis=-1, keepdims=True)
    o_ref[...] = (x * lax.rsqrt(ms + EPS) * g_ref[...]).astype(o_ref.dtype)


def _rmsnorm(x, gains, layer):
    m, d = x.shape
    tr = _tile(m, 256)
    return pl.pallas_call(
        _rmsnorm_kernel,
        grid=(m // tr,),
        in_specs=[pl.BlockSpec((tr, d), lambda i: (i, 0)),
                  pl.BlockSpec((None, 1, d), lambda i: (layer, 0, 0))],
        out_specs=pl.BlockSpec((tr, d), lambda i: (i, 0)),
        out_shape=jax.ShapeDtypeStruct((m, d), ACT_DTYPE),
        compiler_params=pltpu.CompilerParams(
            dimension_semantics=("arbitrary",),
            vmem_limit_bytes=_vmem_limit(_nbytes((tr, d), jnp.float32), _nbytes((tr, d), ACT_DTYPE))),
        name="rmsnorm",
    )(x, gains)


def _gateup_kernel(a_ref, wg_ref, wu_ref, o_ref):
    a = a_ref[...]
    g = _dot(a, wg_ref[...])
    u = _dot(a, wu_ref[...])
    o_ref[...] = (g * _sigmoid(g) * u).astype(o_ref.dtype)


def _gateup(a, w_gate, w_up, layer):
    m, k = a.shape
    f = w_gate.shape[-1]
    tm, tf = _tile(m, 512), _tile(f, 512)
    w_spec = pl.BlockSpec((None, k, tf), lambda j, i: (layer, 0, j))
    return pl.pallas_call(
        _gateup_kernel,
        grid=(f // tf, m // tm),
        in_specs=[pl.BlockSpec((tm, k), lambda j, i: (i, 0)), w_spec, w_spec],
        out_specs=pl.BlockSpec((tm, tf), lambda j, i: (i, j)),
        out_shape=jax.ShapeDtypeStruct((m, f), ACT_DTYPE),
        compiler_params=pltpu.CompilerParams(
            dimension_semantics=("arbitrary", "arbitrary"),
            vmem_limit_bytes=_vmem_limit(_nbytes((tm, k), a.dtype), 2 * _nbytes((k, tf), w_gate.dtype),
                                         _nbytes((tm, tf), jnp.float32))),
        name="ffn_gateup",
    )(a, w_gate, w_up)


def _mm_resid_kernel(a_ref, w_ref, x_ref, o_ref, *, scale):
    o_ref[...] = x_ref[...] + scale * _dot(a_ref[...], w_ref[...])


def _mm_resid(a, w, x, layer, scale, name):
    m, k = a.shape
    n = w.shape[-1]
    tm = _tile(m, 1024 if k <= 4096 else 512)
    tn = _tile(n, 512)
    return pl.pallas_call(
        functools.partial(_mm_resid_kernel, scale=scale),
        grid=(n // tn, m // tm),
        in_specs=[pl.BlockSpec((tm, k), lambda j, i: (i, 0)),
                  pl.BlockSpec((None, k, tn), lambda j, i: (layer, 0, j)),
                  pl.BlockSpec((tm, tn), lambda j, i: (i, j))],
        out_specs=pl.BlockSpec((tm, tn), lambda j, i: (i, j)),
        out_shape=jax.ShapeDtypeStruct((m, n), jnp.float32),
        compiler_params=pltpu.CompilerParams(
            dimension_semantics=("arbitrary", "arbitrary"),
            vmem_limit_bytes=_vmem_limit(_nbytes((tm, k), a.dtype), _nbytes((k, tn), w.dtype),
                                         2 * _nbytes((tm, tn), jnp.float32))),
        name=name,
    )(a, w, x)


def _mm_kernel(a_ref, w_ref, o_ref, *, act):
    y = _dot(a_ref[...], w_ref[...])
    if act == "sigmoid":
        y = _sigmoid(y)
    o_ref[...] = y.astype(o_ref.dtype)


def _mm(a, w, *, layer, col0, n, out_dtype, act, name):
    m, k = a.shape
    tm, tn = _tile(m, 1024), _tile(n, 512)
    assert col0 % tn == 0, (col0, tn)
    jb = col0 // tn
    if layer is None:
        w_spec = pl.BlockSpec((k, tn), lambda j, i: (0, jb + j))
    else:
        w_spec = pl.BlockSpec((None, k, tn), lambda j, i: (layer, 0, jb + j))
    return pl.pallas_call(
        functools.partial(_mm_kernel, act=act),
        grid=(n // tn, m // tm),
        in_specs=[pl.BlockSpec((tm, k), lambda j, i: (i, 0)), w_spec],
        out_specs=pl.BlockSpec((tm, tn), lambda j, i: (i, j)),
        out_shape=jax.ShapeDtypeStruct((m, n), out_dtype),
        compiler_params=pltpu.CompilerParams(
            dimension_semantics=("arbitrary", "arbitrary"),
            vmem_limit_bytes=_vmem_limit(_nbytes((tm, k), a.dtype), _nbytes((k, tn), w.dtype),
                                         _nbytes((tm, tn), jnp.float32))),
        name=name,
    )(a, w)


def _merge_kernel(r_ref, f_ref, l_ref, wr_ref, wf_ref, wl_ref, gr_ref, gf_ref, gl_ref, o_ref):
    y = gr_ref[...].astype(jnp.float32) * _dot(r_ref[...], wr_ref[...])
    y = y + gf_ref[...].astype(jnp.float32) * _dot(f_ref[...], wf_ref[...])
    y = y + gl_ref[...].astype(jnp.float32) * _dot(l_ref[...], wl_ref[...])
    o_ref[...] = y.astype(o_ref.dtype)


def _merge(ret, fox, lru, w_ret, w_fox, w_lru, gates, layer):
    m, kw = ret.shape
    d = w_ret.shape[-1]
    tm, tn = _tile(m, 1024), _tile(d, 512)
    nj = d // tn
    a_spec = pl.BlockSpec((tm, kw), lambda j, i: (i, 0))
    w_spec = pl.BlockSpec((None, kw, tn), lambda j, i: (layer, 0, j))
    g_specs = [pl.BlockSpec((tm, tn), functools.partial(lambda j, i, o: (i, o + j), o=b * nj))
               for b in range(N_BRANCH)]
    return pl.pallas_call(
        _merge_kernel,
        grid=(nj, m // tm),
        in_specs=[a_spec, a_spec, a_spec, w_spec, w_spec, w_spec] + g_specs,
        out_specs=pl.BlockSpec((tm, tn), lambda j, i: (i, j)),
        out_shape=jax.ShapeDtypeStruct((m, d), ACT_DTYPE),
        compiler_params=pltpu.CompilerParams(
            dimension_semantics=("arbitrary", "arbitrary"),
            vmem_limit_bytes=_vmem_limit(3 * _nbytes((tm, kw), ret.dtype), 3 * _nbytes((kw, tn), w_ret.dtype),
                                         4 * _nbytes((tm, tn), jnp.float32))),
        name="branch_merge",
    )(ret, fox, lru, w_ret, w_fox, w_lru, gates, gates, gates)


def _retention_kernel(q_ref, k_ref, v_ref, g_ref, cos_ref, sin_ref, gain_ref, o_ref, *, chunk, n_chunks):
    dh = q_ref.shape[-1]
    half = dh // 2
    f32 = jnp.float32
    h = pl.program_id(1).astype(f32)
    log_gamma = jnp.log(1.0 - jnp.exp2(jnp.full((1, 1), -5.0, f32) - h))
    row = lax.broadcasted_iota(jnp.int32, (chunk, chunk), 0)
    col = lax.broadcasted_iota(jnp.int32, (chunk, chunk), 1)
    rel = (row - col).astype(f32)
    inner_decay = jnp.where(rel >= 0, jnp.exp(jnp.maximum(rel, 0.0) * log_gamma), 0.0)
    idx = lax.broadcasted_iota(jnp.int32, (chunk, 1), 0).astype(f32)
    k_to_end = jnp.exp((chunk - 1.0 - idx) * log_gamma)
    q_from_start = jnp.exp((idx + 1.0) * log_gamma)
    chunk_decay = jnp.exp(chunk * log_gamma)
    gain = gain_ref[...]

    def rotary(x, cos, sin):
        x1, x2 = x[:, :half], x[:, half:]
        return jnp.concatenate([x1 * cos - x2 * sin, x2 * cos + x1 * sin], axis=-1)

    def body(n, state):
        rows = pl.ds(pl.multiple_of(n * chunk, chunk), chunk)
        cos, sin = cos_ref[rows, :], sin_ref[rows, :]
        q = rotary(q_ref[rows, :].astype(f32), cos, sin)
        k = rotary(k_ref[rows, :].astype(f32), cos, sin) * dh ** -0.5
        v = v_ref[rows, :]
        scores = _dot(q.astype(ACT_DTYPE), k.astype(ACT_DTYPE), _NT) * inner_decay
        inner = _dot(scores.astype(ACT_DTYPE), v)
        cross = _dot((q * q_from_start).astype(ACT_DTYPE), state.astype(ACT_DTYPE))
        kv = _dot((k * k_to_end).astype(ACT_DTYPE), v, _TN)
        o = inner + cross
        o = o * lax.rsqrt(jnp.mean(o * o, axis=-1, keepdims=True) + EPS) * gain
        g = g_ref[rows, :].astype(f32)
        o_ref[rows, :] = (g * _sigmoid(g) * o).astype(o_ref.dtype)
        return chunk_decay * state + kv

    lax.fori_loop(0, n_chunks, body, jnp.zeros((dh, dh), f32))


def _retention(p_ret, cos, sin, gains, layer, batch, seq, heads):
    m, w4 = p_ret.shape
    dh = w4 // (4 * heads)
    chunk = _tile(seq, 512)
    blk = lambda seg: pl.BlockSpec((seq, dh), lambda b, h: (b, seg * heads + h))
    tab = pl.BlockSpec((seq, dh // 2), lambda b, h: (0, 0))
    return pl.pallas_call(
        functools.partial(_retention_kernel, chunk=chunk, n_chunks=seq // chunk),
        grid=(batch, heads),
        in_specs=[blk(0), blk(1), blk(2), blk(3), tab, tab,
                  pl.BlockSpec((None, 1, dh), lambda b, h: (layer * heads + h, 0, 0))],
        out_specs=pl.BlockSpec((seq, dh), lambda b, h: (b, h)),
        out_shape=jax.ShapeDtypeStruct((m, heads * dh), ACT_DTYPE),
        compiler_params=pltpu.CompilerParams(
            dimension_semantics=("arbitrary", "arbitrary"),
            vmem_limit_bytes=_vmem_limit(5 * _nbytes((seq, dh), p_ret.dtype), 2 * _nbytes((seq, dh // 2), jnp.float32),
                                         scratch_bytes=8 * _nbytes((chunk, max(chunk, dh)), jnp.float32))),
        name="retention",
    )(p_ret, p_ret, p_ret, p_ret, cos, sin, gains)


def _logf_cumsum_kernel(fg_ref, bias_ref, o_ref, *, chunk, n_chunks):
    f32 = jnp.float32
    row = lax.broadcasted_iota(jnp.int32, (chunk, chunk), 0)
    col = lax.broadcasted_iota(jnp.int32, (chunk, chunk), 1)
    upper = (row <= col).astype(ACT_DTYPE)
    bias = bias_ref[...]

    def body(n, carry):
        rows = pl.ds(pl.multiple_of(n * chunk, chunk), chunk)
        log_f = -_softplus(-(fg_ref[rows, :] + bias))
        p1 = log_f.astype(ACT_DTYPE)
        r1 = log_f - p1.astype(f32)
        p2 = r1.astype(ACT_DTYPE)
        p3 = (r1 - p2.astype(f32)).astype(ACT_DTYPE)
        cs = _dot(p1, upper, _TN) + _dot(p2, upper, _TN) + _dot(p3, upper, _TN)
        cs = cs + carry
        o_ref[:, rows] = cs
        return cs[:, chunk - 1:chunk]

    lax.fori_loop(0, n_chunks, body, jnp.zeros((fg_ref.shape[-1], 1), f32))


def _logf_cumsum(fg, bias, layer, batch, seq):
    lanes = fg.shape[-1]
    chunk = _tile(seq, 512)
    return pl.pallas_call(
        functools.partial(_logf_cumsum_kernel, chunk=chunk, n_chunks=seq // chunk),
        grid=(batch,),
        in_specs=[pl.BlockSpec((seq, lanes), lambda b: (b, 0)),
                  pl.BlockSpec((None, 1, lanes), lambda b: (layer, 0, 0))],
        out_specs=pl.BlockSpec((None, lanes, seq), lambda b: (b, 0, 0)),
        out_shape=jax.ShapeDtypeStruct((batch, lanes, seq), jnp.float32),
        compiler_params=pltpu.CompilerParams(
            dimension_semantics=("arbitrary",),
            vmem_limit_bytes=_vmem_limit(2 * _nbytes((seq, lanes), jnp.float32))),
        name="fox_logf_cumsum",
    )(fg, bias)


def _head_rms(x, gain):
    return x * lax.rsqrt(jnp.mean(x * x, axis=-1, keepdims=True) + EPS) * gain


def _fox_kernel(q_ref, k_ref, v_ref, cum_ref, qg_ref, kg_ref, o_ref, kn_ref, m_ref, l_ref, acc_ref, *, tq, scale):
    f32 = jnp.float32
    qi = pl.program_id(2)

    @pl.when(qi == 0)
    def _():
        kn_ref[...] = _head_rms(k_ref[...].astype(f32), kg_ref[...]).astype(kn_ref.dtype)

    q = _head_rms(q_ref[...].astype(f32), qg_ref[...]).astype(ACT_DTYPE)
    m_ref[...] = jnp.full(m_ref.shape, -jnp.inf, f32)
    l_ref[...] = jnp.zeros(l_ref.shape, f32)
    acc_ref[...] = jnp.zeros(acc_ref.shape, f32)

    def step(j, masked):
        cols = pl.ds(pl.multiple_of(j * tq, tq), tq)
        s = _dot(q, kn_ref[cols, :], _NT) * scale - cum_ref[:, cols]
        if masked:
            r = lax.broadcasted_iota(jnp.int32, (tq, tq), 0)
            c = lax.broadcasted_iota(jnp.int32, (tq, tq), 1)
            s = jnp.where(r >= c, s, -jnp.inf)
        m_prev = m_ref[...]
        m_new = jnp.maximum(m_prev, jnp.max(s, axis=-1, keepdims=True))
        alpha = jnp.exp(m_prev - m_new)
        p = jnp.exp(s - m_new)
        l_ref[...] = alpha * l_ref[...] + jnp.sum(p, axis=-1, keepdims=True)
        acc_ref[...] = alpha * acc_ref[...] + _dot(p.astype(ACT_DTYPE), v_ref[cols, :])
        m_ref[...] = m_new

    def body(j, carry):
        step(j, masked=False)
        return carry

    lax.fori_loop(0, qi, body, 0)
    step(qi, masked=True)
    o_ref[...] = (acc_ref[...] / l_ref[...]).astype(o_ref.dtype)


def _fox(p_fox, cum, q_gain, k_gain, layer, batch, seq, heads):
    m, w3 = p_fox.shape
    dh = w3 // (3 * heads)
    tq = _tile(seq, 512)
    nq = seq // tq
    gain = pl.BlockSpec((None, 1, dh), lambda b, h, i: (layer, 0, 0))
    return pl.pallas_call(
        functools.partial(_fox_kernel, tq=tq, scale=dh ** -0.5),
        grid=(batch, heads, nq),
        in_specs=[pl.BlockSpec((tq, dh), lambda b, h, i: (b * nq + i, h)),
                  pl.BlockSpec((seq, dh), lambda b, h, i: (b, heads + h)),
                  pl.BlockSpec((seq, dh), lambda b, h, i: (b, 2 * heads + h)),
                  pl.BlockSpec((None, 1, seq), lambda b, h, i: (b * heads + h, 0, 0)),
                  gain, gain],
        out_specs=pl.BlockSpec((tq, dh), lambda b, h, i: (b * nq + i, h)),
        out_shape=jax.ShapeDtypeStruct((m, heads * dh), ACT_DTYPE),
        scratch_shapes=[pltpu.VMEM((seq, dh), ACT_DTYPE),
                        pltpu.VMEM((tq, 1), jnp.float32),
                        pltpu.VMEM((tq, 1), jnp.float32),
                        pltpu.VMEM((tq, dh), jnp.float32)],
        compiler_params=pltpu.CompilerParams(
            dimension_semantics=("arbitrary", "arbitrary", "arbitrary"),
            vmem_limit_bytes=_vmem_limit(3 * _nbytes((seq, dh), p_fox.dtype),
                                         scratch_bytes=8 * _nbytes((tq, tq), jnp.float32))),
        name="forgetting_attention",
    )(p_fox, p_fox, p_fox, cum, q_gain, k_gain)


def _shift_rows(x, s, row):
    return jnp.where(row >= s, pltpu.roll(x, s, axis=0), 0.0)


def _lru_kernel(gate_ref, x_ref, cw_ref, prm_ref, wa_ref, wx_ref, o_ref, *, conv_width):
    f32 = jnp.float32
    seq, width = x_ref.shape
    row = lax.broadcasted_iota(jnp.int32, (seq, width), 0)
    x = x_ref[...].astype(f32)
    conv_b, b_a, b_x, lam = (prm_ref[i:i + 1, :] for i in range(4))
    xc = conv_b + x * cw_ref[conv_width - 1:conv_width, :]
    for s in range(1, conv_width):
        xc = xc + _shift_rows(x, s, row) * cw_ref[conv_width - 1 - s:conv_width - s, :]
    xcb = xc.astype(ACT_DTYPE)
    r = _sigmoid(_dot(xcb, wa_ref[...]) + b_a)
    i = _sigmoid(_dot(xcb, wx_ref[...]) + b_x)
    log_a = -LRU_C * r * _softplus(-lam)
    a = jnp.exp(log_a)
    u = jnp.sqrt(-jnp.tanh(log_a) * (a * a + 1.0)) * (i * xc)
    s = 1
    while s < seq:
        keep = row >= s
        u = jnp.where(keep, a * pltpu.roll(u, s, axis=0) + u, u)
        a = jnp.where(keep, a * pltpu.roll(a, s, axis=0), a)
        s *= 2
    o_ref[...] = (jax.nn.gelu(gate_ref[...].astype(f32)) * u).astype(o_ref.dtype)


def _lru(p_lru, conv_w, prm, w_a, w_x, layer, batch, seq):
    m, w2 = p_lru.shape
    nblk, bw = w_a.shape[1], w_a.shape[2]
    conv_width = conv_w.shape[1]
    wmat = pl.BlockSpec((None, None, bw, bw), lambda b, c: (layer, c, 0, 0))
    return pl.pallas_call(
        functools.partial(_lru_kernel, conv_width=conv_width),
        grid=(batch, nblk),
        in_specs=[pl.BlockSpec((seq, bw), lambda b, c: (b, c)),
                  pl.BlockSpec((seq, bw), lambda b, c: (b, nblk + c)),
                  pl.BlockSpec((None, conv_width, bw), lambda b, c: (layer, 0, c)),
                  pl.BlockSpec((None, 8, bw), lambda b, c: (layer, 0, c)),
                  wmat, wmat],
        out_specs=pl.BlockSpec((seq, bw), lambda b, c: (b, c)),
        out_shape=jax.ShapeDtypeStruct((m, w2 // 2), ACT_DTYPE),
        compiler_params=pltpu.CompilerParams(
            dimension_semantics=("arbitrary", "arbitrary"),
            vmem_limit_bytes=_vmem_limit(3 * _nbytes((seq, bw), p_lru.dtype),
                                         scratch_bytes=12 * _nbytes((seq, bw), jnp.float32))),
        name="rg_lru",
    )(p_lru, p_lru, conv_w, prm, w_a, w_x)


def kernel(x, ffn1_norm, ffn1_w_gate, ffn1_w_up, ffn1_w_down, mix_norm, w_in, ret_norm, fox_q_norm, fox_k_norm, fox_f_bias, lru_conv_w, lru_conv_b, lru_w_a, lru_b_a, lru_w_x, lru_b_x, lru_lambda, w_branch_ret, w_branch_fox, w_branch_lru, w_out, ffn2_norm, ffn2_w_gate, ffn2_w_up, ffn2_w_down):
    batch, seq, d = x.shape
    depth = w_in.shape[0]
    ret_heads, ret_dh = ret_norm.shape[1], ret_norm.shape[2]
    fox_heads, fox_dh = fox_f_bias.shape[1], fox_q_norm.shape[1]
    ret_w, fox_w, lru_w = ret_heads * ret_dh, fox_heads * fox_dh, lru_lambda.shape[1]
    assert fox_heads <= V7X_LANES
    c_fox = 4 * ret_w
    c_fg = c_fox + 3 * fox_w
    c_tail = c_fg + fox_heads
    assert w_in.shape[-1] == c_tail + 2 * lru_w + N_BRANCH * d

    f32 = jnp.float32
    as_rows = lambda p: p.reshape(depth, 1, -1)
    ffn1_g, mix_g, ffn2_g = as_rows(ffn1_norm), as_rows(mix_norm), as_rows(ffn2_norm)
    fox_qg, fox_kg = as_rows(fox_q_norm), as_rows(fox_k_norm)
    ret_g = ret_norm.reshape(depth * ret_heads, 1, ret_dh)
    fg_bias = as_rows(jnp.pad(fox_f_bias, ((0, 0), (0, V7X_LANES - fox_heads))))
    zeros = jnp.zeros_like(lru_lambda)
    lru_prm = jnp.stack([lru_conv_b, lru_b_a, lru_b_x, lru_lambda, zeros, zeros, zeros, zeros], axis=1)

    half = ret_dh // 2
    inv_freq = ROPE_BASE ** (-jnp.arange(half, dtype=f32) / half)
    ang = jnp.arange(seq).astype(f32)[:, None] * inv_freq[None, :]
    cos, sin = jnp.cos(ang), jnp.sin(ang)

    x = x.reshape(batch * seq, d)
    for l in range(depth):
        h = _gateup(_rmsnorm(x, ffn1_g, l), ffn1_w_gate, ffn1_w_up, l)
        x = _mm_resid(h, ffn1_w_down, x, l, HALF_STEP, "ffn_down")

        hn = _rmsnorm(x, mix_g, l)
        w_fg = jnp.pad(w_in[l, :, c_fg:c_tail], ((0, 0), (0, V7X_LANES - fox_heads)))
        w_tail = w_in[l, :, c_tail:].astype(ACT_DTYPE)
        p_ret = _mm(hn, w_in, layer=l, col0=0, n=4 * ret_w, out_dtype=ACT_DTYPE, act=None, name="proj_ret")
        p_fox = _mm(hn, w_in, layer=l, col0=c_fox, n=3 * fox_w, out_dtype=ACT_DTYPE, act=None, name="proj_fox")
        fg = _mm(hn, w_fg, layer=None, col0=0, n=V7X_LANES, out_dtype=f32, act=None, name="proj_fgate")
        p_lru = _mm(hn, w_tail, layer=None, col0=0, n=2 * lru_w, out_dtype=ACT_DTYPE, act=None, name="proj_lru")
        gates = _mm(hn, w_tail, layer=None, col0=2 * lru_w, n=N_BRANCH * d, out_dtype=ACT_DTYPE, act="sigmoid",
                    name="proj_gates")

        ret = _retention(p_ret, cos, sin, ret_g, l, batch, seq, ret_heads)
        cum = _logf_cumsum(fg, fg_bias, l, batch, seq)[:, :fox_heads].reshape(batch * fox_heads, 1, seq)
        fox = _fox(p_fox, cum, fox_qg, fox_kg, l, batch, seq, fox_heads)
        lru = _lru(p_lru, lru_conv_w, lru_prm, lru_w_a, lru_w_x, l, batch, seq)

        mixed = _merge(ret, fox, lru, w_branch_ret, w_branch_fox, w_branch_lru, gates, l)
        x = _mm_resid(mixed, w_out, x, l, 1.0, "out_proj")

        h = _gateup(_rmsnorm(x, ffn2_g, l), ffn2_w_gate, ffn2_w_up, l)
        x = _mm_resid(h, ffn2_w_down, x, l, HALF_STEP, "ffn_down")
    return x.reshape(batch, seq, d)
```

```python
import functools
import math

import jax
import jax.numpy as jnp
from jax import lax
from jax.experimental import pallas as pl
from jax.experimental.pallas import tpu as pltpu

EPS = 1e-6
HALF_STEP = 0.5
ROPE_BASE = 10000.0
LRU_C = 8.0
N_BRANCH = 3
LOG2_E = 1.4426950408889634

V7X_LANES = 128
V7X_VMEM_REQUEST_CAP = 60000 * 1024
ACT_DTYPE = jnp.bfloat16


def _vmem_limit(*block_bytes, scratch_bytes=0):
    need = 3 * sum(block_bytes) + scratch_bytes
    return int(min(max(need, 16 * 1024 * 1024), V7X_VMEM_REQUEST_CAP))


def _nbytes(shape, dtype):
    return math.prod(shape) * jnp.dtype(dtype).itemsize


def _tile(n, pref):
    t = min(n, pref)
    assert n % t == 0, (n, pref)
    return t


def _dot(a, b, dims=(((1,), (0,)), ((), ()))):
    return lax.dot_general(a, b, dims, preferred_element_type=jnp.float32)


_NT = (((1,), (1,)), ((), ()))
_TN = (((0,), (0,)), ((), ()))


def _sigmoid(x):
    return 1.0 / (1.0 + jnp.exp(-x))


def _softplus(x):
    return jnp.maximum(x, 0.0) + jnp.log1p(jnp.exp(-jnp.abs(x)))


def _rmsnorm_kernel(x_ref, g_ref, o_ref):
    x = x_ref[...]
    ms = jnp.mean(x * x, axis=-1, keepdims=True)
    o_ref[...] = (x * lax.rsqrt(ms + EPS) * g_ref[...]).astype(o_ref.dtype)


def _rmsnorm(x, gains, layer):
    m, d = x.shape
    tr = _tile(m, 256)
    return pl.pallas_call(
        _rmsnorm_kernel,
        grid=(m // tr,),
        in_specs=[pl.BlockSpec((tr, d), lambda i: (i, 0)),
                  pl.BlockSpec((None, 1, d), lambda i: (layer, 0, 0))],
        out_specs=pl.BlockSpec((tr, d), lambda i: (i, 0)),
        out_shape=jax.ShapeDtypeStruct((m, d), ACT_DTYPE),
        compiler_params=pltpu.CompilerParams(
            dimension_semantics=("arbitrary",),
            vmem_limit_bytes=_vmem_limit(_nbytes((tr, d), jnp.float32), _nbytes((tr, d), ACT_DTYPE))),
        name="rmsnorm",
    )(x, gains)


def _gateup_kernel(a_ref, wg_ref, wu_ref, o_ref):
    a = a_ref[...]
    g = _dot(a, wg_ref[...])
    u = _dot(a, wu_ref[...])
    o_ref[...] = (g * _sigmoid(g) * u).astype(o_ref.dtype)


def _ordered(n_i, n_j, weights_outer):
    if weights_outer:
        return (n_j, n_i), lambda f: (lambda j, i: f(i, j))
    return (n_i, n_j), lambda f: f


def _gateup(a, w_gate, w_up, layer, tm=512, tf=512, weights_outer=True):
    m, k = a.shape
    f = w_gate.shape[-1]
    tm, tf = _tile(m, tm), _tile(f, tf)
    grid, ij = _ordered(m // tm, f // tf, weights_outer)
    w_spec = pl.BlockSpec((None, k, tf), ij(lambda i, j: (layer, 0, j)))
    return pl.pallas_call(
        _gateup_kernel,
        grid=grid,
        in_specs=[pl.BlockSpec((tm, k), ij(lambda i, j: (i, 0))), w_spec, w_spec],
        out_specs=pl.BlockSpec((tm, tf), ij(lambda i, j: (i, j))),
        out_shape=jax.ShapeDtypeStruct((m, f), ACT_DTYPE),
        compiler_params=pltpu.CompilerParams(
            dimension_semantics=("arbitrary", "arbitrary"),
            vmem_limit_bytes=_vmem_limit(_nbytes((tm, k), a.dtype), 2 * _nbytes((k, tf), w_gate.dtype),
                                         _nbytes((tm, tf), jnp.float32))),
        name="ffn_gateup",
    )(a, w_gate, w_up)


def _mm_resid_kernel(a_ref, w_ref, x_ref, o_ref, *, scale):
    o_ref[...] = x_ref[...] + scale * _dot(a_ref[...], w_ref[...])


def _mm_resid(a, w, x, layer, scale, name, tm=None, tn=512, weights_outer=True):
    m, k = a.shape
    n = w.shape[-1]
    tm = _tile(m, tm or (1024 if k <= 4096 else 512))
    tn = _tile(n, tn)
    grid, ij = _ordered(m // tm, n // tn, weights_outer)
    return pl.pallas_call(
        functools.partial(_mm_resid_kernel, scale=scale),
        grid=grid,
        in_specs=[pl.BlockSpec((tm, k), ij(lambda i, j: (i, 0))),
                  pl.BlockSpec((None, k, tn), ij(lambda i, j: (layer, 0, j))),
                  pl.BlockSpec((tm, tn), ij(lambda i, j: (i, j)))],
        out_specs=pl.BlockSpec((tm, tn), ij(lambda i, j: (i, j))),
        out_shape=jax.ShapeDtypeStruct((m, n), jnp.float32),
        compiler_params=pltpu.CompilerParams(
            dimension_semantics=("arbitrary", "arbitrary"),
            vmem_limit_bytes=_vmem_limit(_nbytes((tm, k), a.dtype), _nbytes((k, tn), w.dtype),
                                         2 * _nbytes((tm, tn), jnp.float32))),
        name=name,
    )(a, w, x)


def _mm_nt_kernel(a_ref, wt_ref, o_ref, *, act):
    y = _dot(a_ref[...], wt_ref[0], _NT)
    if act == "sigmoid":
        y = _sigmoid(y)
    o_ref[...] = y.astype(o_ref.dtype)


def _mm_nt(a, wt, *, layer, row0, n, out_dtype, act, name, weights_outer=True):
    m, k = a.shape
    tm, tn = _tile(m, 1024), _tile(n, 512)
    grid, ij = _ordered(m // tm, n // tn, weights_outer)
    return pl.pallas_call(
        functools.partial(_mm_nt_kernel, act=act),
        grid=grid,
        in_specs=[pl.BlockSpec((tm, k), ij(lambda i, j: (i, 0))),
                  pl.BlockSpec((pl.Element(1), pl.Element(tn), pl.Element(k)),
                               ij(lambda i, j: (layer, pl.multiple_of(row0 + j * tn, 8), 0)))],
        out_specs=pl.BlockSpec((tm, tn), ij(lambda i, j: (i, j))),
        out_shape=jax.ShapeDtypeStruct((m, n), out_dtype),
        compiler_params=pltpu.CompilerParams(
            dimension_semantics=("arbitrary", "arbitrary"),
            vmem_limit_bytes=_vmem_limit(_nbytes((tm, k), a.dtype), _nbytes((tn, k), wt.dtype),
                                         _nbytes((tm, tn), jnp.float32))),
        name=name,
    )(a, wt)


def _merge_kernel(r_ref, f_ref, l_ref, wr_ref, wf_ref, wl_ref, gr_ref, gf_ref, gl_ref, o_ref):
    y = gr_ref[...].astype(jnp.float32) * _dot(r_ref[...], wr_ref[...])
    y = y + gf_ref[...].astype(jnp.float32) * _dot(f_ref[...], wf_ref[...])
    y = y + gl_ref[...].astype(jnp.float32) * _dot(l_ref[...], wl_ref[...])
    o_ref[...] = y.astype(o_ref.dtype)


def _merge(ret, fox, lru, w_ret, w_fox, w_lru, gates, layer):
    m, kw = ret.shape
    d = w_ret.shape[-1]
    tm, tn = _tile(m, 1024), _tile(d, 512)
    nj = d // tn
    a_spec = pl.BlockSpec((tm, kw), lambda j, i: (i, 0))
    w_spec = pl.BlockSpec((None, kw, tn), lambda j, i: (layer, 0, j))
    g_specs = [pl.BlockSpec((tm, tn), functools.partial(lambda j, i, o: (i, o + j), o=b * nj))
               for b in range(N_BRANCH)]
    return pl.pallas_call(
        _merge_kernel,
        grid=(nj, m // tm),
        in_specs=[a_spec, a_spec, a_spec, w_spec, w_spec, w_spec] + g_specs,
        out_specs=pl.BlockSpec((tm, tn), lambda j, i: (i, j)),
        out_shape=jax.ShapeDtypeStruct((m, d), ACT_DTYPE),
        compiler_params=pltpu.CompilerParams(
            dimension_semantics=("arbitrary", "arbitrary"),
            vmem_limit_bytes=_vmem_limit(3 * _nbytes((tm, kw), ret.dtype), 3 * _nbytes((kw, tn), w_ret.dtype),
                                         4 * _nbytes((tm, tn), jnp.float32))),
        name="branch_merge",
    )(ret, fox, lru, w_ret, w_fox, w_lru, gates, gates, gates)


def _retention_kernel(q_ref, k_ref, v_ref, g_ref, cos_ref, sin_ref, gain_ref, o_ref, *, chunk, n_chunks):
    dh = q_ref.shape[-1]
    half = dh // 2
    f32 = jnp.float32
    h = pl.program_id(1).astype(f32)
    log_gamma = jnp.log(1.0 - jnp.exp2(jnp.full((1, 1), -5.0, f32) - h))
    row = lax.broadcasted_iota(jnp.int32, (chunk, chunk), 0)
    col = lax.broadcasted_iota(jnp.int32, (chunk, chunk), 1)
    rel = (row - col).astype(f32)
    inner_decay = jnp.where(rel >= 0, jnp.exp(jnp.maximum(rel, 0.0) * log_gamma), 0.0)
    idx = lax.broadcasted_iota(jnp.int32, (chunk, 1), 0).astype(f32)
    k_to_end = jnp.exp((chunk - 1.0 - idx) * log_gamma)
    q_from_start = jnp.exp((idx + 1.0) * log_gamma)
    chunk_decay = jnp.exp(chunk * log_gamma)
    gain = gain_ref[...]

    def rotary(x, cos, sin):
        x1, x2 = x[:, :half], x[:, half:]
        return jnp.concatenate([x1 * cos - x2 * sin, x2 * cos + x1 * sin], axis=-1)

    def body(n, state):
        rows = pl.ds(pl.multiple_of(n * chunk, chunk), chunk)
        cos, sin = cos_ref[rows, :], sin_ref[rows, :]
        q = rotary(q_ref[rows, :].astype(f32), cos, sin)
        k = rotary(k_ref[rows, :].astype(f32), cos, sin) * dh ** -0.5
        v = v_ref[rows, :]
        scores = _dot(q.astype(ACT_DTYPE), k.astype(ACT_DTYPE), _NT) * inner_decay
        inner = _dot(scores.astype(ACT_DTYPE), v)
        cross = _dot((q * q_from_start).astype(ACT_DTYPE), state.astype(ACT_DTYPE))
        kv = _dot((k * k_to_end).astype(ACT_DTYPE), v, _TN)
        o = inner + cross
        o = o * lax.rsqrt(jnp.mean(o * o, axis=-1, keepdims=True) + EPS) * gain
        g = g_ref[rows, :].astype(f32)
        o_ref[rows, :] = (g * _sigmoid(g) * o).astype(o_ref.dtype)
        return chunk_decay * state + kv

    lax.fori_loop(0, n_chunks, body, jnp.zeros((dh, dh), f32))


def _retention(p_ret, cos, sin, gains, layer, batch, seq, heads):
    m, w4 = p_ret.shape
    dh = w4 // (4 * heads)
    chunk = _tile(seq, 512)
    blk = lambda seg: pl.BlockSpec((seq, dh), lambda b, h: (b, seg * heads + h))
    tab = pl.BlockSpec((seq, dh // 2), lambda b, h: (0, 0))
    return pl.pallas_call(
        functools.partial(_retention_kernel, chunk=chunk, n_chunks=seq // chunk),
        grid=(batch, heads),
        in_specs=[blk(0), blk(1), blk(2), blk(3), tab, tab,
                  pl.BlockSpec((None, 1, dh), lambda b, h: (layer * heads + h, 0, 0))],
        out_specs=pl.BlockSpec((seq, dh), lambda b, h: (b, h)),
        out_shape=jax.ShapeDtypeStruct((m, heads * dh), ACT_DTYPE),
        compiler_params=pltpu.CompilerParams(
            dimension_semantics=("arbitrary", "arbitrary"),
            vmem_limit_bytes=_vmem_limit(5 * _nbytes((seq, dh), p_ret.dtype), 2 * _nbytes((seq, dh // 2), jnp.float32),
                                         scratch_bytes=8 * _nbytes((chunk, max(chunk, dh)), jnp.float32))),
        name="retention",
    )(p_ret, p_ret, p_ret, p_ret, cos, sin, gains)


def _logf_cumsum_kernel(fg_ref, bias_ref, o_ref, *, chunk, n_chunks):
    f32 = jnp.float32
    row = lax.broadcasted_iota(jnp.int32, (chunk, chunk), 0)
    col = lax.broadcasted_iota(jnp.int32, (chunk, chunk), 1)
    upper = (row <= col).astype(ACT_DTYPE)
    bias = bias_ref[...]

    def body(n, carry):
        rows = pl.ds(pl.multiple_of(n * chunk, chunk), chunk)
        log_f = -_softplus(-(fg_ref[rows, :] + bias))
        p1 = log_f.astype(ACT_DTYPE)
        r1 = log_f - p1.astype(f32)
        p2 = r1.astype(ACT_DTYPE)
        p3 = (r1 - p2.astype(f32)).astype(ACT_DTYPE)
        cs = _dot(p1, upper, _TN) + _dot(p2, upper, _TN) + _dot(p3, upper, _TN)
        cs = cs + carry
        o_ref[:, rows] = cs
        return cs[:, chunk - 1:chunk]

    lax.fori_loop(0, n_chunks, body, jnp.zeros((fg_ref.shape[-1], 1), f32))


def _logf_cumsum(fg, bias, layer, batch, seq):
    lanes = fg.shape[-1]
    chunk = _tile(seq, 512)
    return pl.pallas_call(
        functools.partial(_logf_cumsum_kernel, chunk=chunk, n_chunks=seq // chunk),
        grid=(batch,),
        in_specs=[pl.BlockSpec((seq, lanes), lambda b: (b, 0)),
                  pl.BlockSpec((None, 1, lanes), lambda b: (layer, 0, 0))],
        out_specs=pl.BlockSpec((None, lanes, seq), lambda b: (b, 0, 0)),
        out_shape=jax.ShapeDtypeStruct((batch, lanes, seq), jnp.float32),
        compiler_params=pltpu.CompilerParams(
            dimension_semantics=("arbitrary",),
            vmem_limit_bytes=_vmem_limit(2 * _nbytes((seq, lanes), jnp.float32))),
        name="fox_logf_cumsum",
    )(fg, bias)


def _head_rms(x, gain):
    return x * lax.rsqrt(jnp.mean(x * x, axis=-1, keepdims=True) + EPS) * gain


FOX_HEADS_PER_STEP = 2


def _fox_kernel(q_ref, k_ref, v_ref, cum_ref, qg_ref, kg_ref, o_ref, kn_ref, *, tq, nq, dh, scale):
    f32 = jnp.float32
    qi = pl.program_id(2)
    head_cols = [slice(hh * dh, (hh + 1) * dh) for hh in range(q_ref.shape[-1] // dh)]

    @pl.when(qi == 0)
    def _():
        for cs in head_cols:
            kn_ref[:, cs] = (_head_rms(k_ref[:, cs].astype(f32), kg_ref[...]) * (scale * LOG2_E)).astype(kn_ref.dtype)

    causal = (lax.broadcasted_iota(jnp.int32, (tq, tq), 0) >= lax.broadcasted_iota(jnp.int32, (tq, tq), 1))

    def one_head(hh, cs, lo):
        q = _head_rms(q_ref[:, cs].astype(f32), qg_ref[...]).astype(ACT_DTYPE)

        def logits(a, b):
            return _dot(q, kn_ref[a:b, cs], _NT) - cum_ref[hh:hh + 1, a:b] * LOG2_E

        s_diag = jnp.where(causal, logits(lo, lo + tq), -jnp.inf)
        m = jnp.max(s_diag, axis=-1, keepdims=True)
        if lo:
            s_past = logits(0, lo)
            m = jnp.maximum(m, jnp.max(s_past, axis=-1, keepdims=True))
        p = jnp.exp2(s_diag - m)
        l = jnp.sum(p, axis=-1, keepdims=True)
        acc = _dot(p.astype(ACT_DTYPE), v_ref[lo:lo + tq, cs])
        if lo:
            p = jnp.exp2(s_past - m)
            l = l + jnp.sum(p, axis=-1, keepdims=True)
            acc = acc + _dot(p.astype(ACT_DTYPE), v_ref[0:lo, cs])
        o_ref[:, cs] = (acc / l).astype(o_ref.dtype)

    for case in range(nq):
        @pl.when(qi == case)
        def _(lo=case * tq):
            for hh, cs in enumerate(head_cols):
                one_head(hh, cs, lo)


def _fox(p_fox, cum, q_gain, k_gain, layer, batch, seq, heads):
    m, w3 = p_fox.shape
    dh = w3 // (3 * heads)
    hp = FOX_HEADS_PER_STEP if heads % FOX_HEADS_PER_STEP == 0 else 1
    hg = heads // hp
    tq = _tile(seq, 512)
    nq = seq // tq
    cum = cum.reshape(batch * hg, hp, seq)
    gain = pl.BlockSpec((None, 1, dh), lambda b, g, i: (layer, 0, 0))
    return pl.pallas_call(
        functools.partial(_fox_kernel, tq=tq, nq=nq, dh=dh, scale=dh ** -0.5),
        grid=(batch, hg, nq),
        in_specs=[pl.BlockSpec((tq, hp * dh), lambda b, g, i: (b * nq + i, g)),
                  pl.BlockSpec((seq, hp * dh), lambda b, g, i: (b, hg + g)),
                  pl.BlockSpec((seq, hp * dh), lambda b, g, i: (b, 2 * hg + g)),
                  pl.BlockSpec((None, hp, seq), lambda b, g, i: (b * hg + g, 0, 0)),
                  gain, gain],
        out_specs=pl.BlockSpec((tq, hp * dh), lambda b, g, i: (b * nq + i, g)),
        out_shape=jax.ShapeDtypeStruct((m, heads * dh), ACT_DTYPE),
        scratch_shapes=[pltpu.VMEM((seq, hp * dh), ACT_DTYPE)],
        compiler_params=pltpu.CompilerParams(
            dimension_semantics=("arbitrary", "arbitrary", "arbitrary"),
            vmem_limit_bytes=_vmem_limit(3 * _nbytes((seq, hp * dh), p_fox.dtype),
                                         scratch_bytes=4 * hp * _nbytes((tq, seq), jnp.float32))),
        name="forgetting_attention",
    )(p_fox, p_fox, p_fox, cum, q_gain, k_gain)


LRU_PAD_ROWS = 8


def _lru_kernel(gate_ref, x_ref, cw_ref, prm_ref, wa_ref, wx_ref, o_ref, xs_ref, us_ref, as_ref, *, conv_width):
    f32 = jnp.float32
    seq, width = x_ref.shape
    pad = LRU_PAD_ROWS
    assert conv_width - 1 <= pad
    body = pl.ds(pad, seq)

    def shifted(ref, s):
        return ref[pl.ds(pad - s, seq), :]

    x = x_ref[...].astype(f32)
    xs_ref[0:pad, :] = jnp.zeros((pad, width), f32)
    xs_ref[body, :] = x
    conv_b, b_a, b_x, lam = (prm_ref[i:i + 1, :] for i in range(4))
    xc = conv_b + x * cw_ref[conv_width - 1:conv_width, :]
    for s in range(1, conv_width):
        xc = xc + shifted(xs_ref, s) * cw_ref[conv_width - 1 - s:conv_width - s, :]
    xcb = xc.astype(ACT_DTYPE)
    r = _sigmoid(_dot(xcb, wa_ref[...]) + b_a)
    i = _sigmoid(_dot(xcb, wx_ref[...]) + b_x)
    log_a = -LRU_C * r * _softplus(-lam)
    a = jnp.exp(log_a)
    u = jnp.sqrt(-jnp.tanh(log_a) * (a * a + 1.0)) * (i * xc)
    us_ref[0:pad, :] = jnp.zeros((pad, width), f32)
    as_ref[0:pad, :] = jnp.ones((pad, width), f32)
    s = 1
    while s < seq:
        last = 2 * s >= seq
        if s < pad:
            us_ref[body, :] = u
            as_ref[body, :] = a
            u = a * shifted(us_ref, s) + u
            a = a if last else a * shifted(as_ref, s)
        else:
            u = jnp.concatenate([u[:s], a[s:] * u[:-s] + u[s:]], axis=0)
            a = a if last else jnp.concatenate([a[:s], a[s:] * a[:-s]], axis=0)
        s *= 2
    o_ref[...] = (jax.nn.gelu(gate_ref[...].astype(f32)) * u).astype(o_ref.dtype)


def _lru(p_lru, conv_w, prm, w_a, w_x, layer, batch, seq):
    m, w2 = p_lru.shape
    nblk, bw = w_a.shape[1], w_a.shape[2]
    conv_width = conv_w.shape[1]
    wmat = pl.BlockSpec((None, None, bw, bw), lambda b, c: (layer, c, 0, 0))
    return pl.pallas_call(
        functools.partial(_lru_kernel, conv_width=conv_width),
        grid=(batch, nblk),
        in_specs=[pl.BlockSpec((seq, bw), lambda b, c: (b, c)),
                  pl.BlockSpec((seq, bw), lambda b, c: (b, nblk + c)),
                  pl.BlockSpec((None, conv_width, bw), lambda b, c: (layer, 0, c)),
                  pl.BlockSpec((None, 8, bw), lambda b, c: (layer, 0, c)),
                  wmat, wmat],
        out_specs=pl.BlockSpec((seq, bw), lambda b, c: (b, c)),
        out_shape=jax.ShapeDtypeStruct((m, w2 // 2), ACT_DTYPE),
        scratch_shapes=[pltpu.VMEM((LRU_PAD_ROWS + seq, bw), jnp.float32)] * 3,
        compiler_params=pltpu.CompilerParams(
            dimension_semantics=("arbitrary", "arbitrary"),
            vmem_limit_bytes=_vmem_limit(3 * _nbytes((seq, bw), p_lru.dtype),
                                         scratch_bytes=15 * _nbytes((seq, bw), jnp.float32))),
        name="rg_lru",
    )(p_lru, p_lru, conv_w, prm, w_a, w_x)


def kernel(x, ffn1_norm, ffn1_w_gate, ffn1_w_up, ffn1_w_down, mix_norm, w_in, ret_norm, fox_q_norm, fox_k_norm, fox_f_bias, lru_conv_w, lru_conv_b, lru_w_a, lru_b_a, lru_w_x, lru_b_x, lru_lambda, w_branch_ret, w_branch_fox, w_branch_lru, w_out, ffn2_norm, ffn2_w_gate, ffn2_w_up, ffn2_w_down):
    batch, seq, d = x.shape
    depth = w_in.shape[0]
    ret_heads, ret_dh = ret_norm.shape[1], ret_norm.shape[2]
    fox_heads, fox_dh = fox_f_bias.shape[1], fox_q_norm.shape[1]
    ret_w, fox_w, lru_w = ret_heads * ret_dh, fox_heads * fox_dh, lru_lambda.shape[1]
    assert fox_heads <= V7X_LANES
    c_fox = 4 * ret_w
    c_fg = c_fox + 3 * fox_w
    c_tail = c_fg + fox_heads
    assert w_in.shape[-1] == c_tail + 2 * lru_w + N_BRANCH * d

    f32 = jnp.float32
    as_rows = lambda p: p.reshape(depth, 1, -1)
    ffn1_g, mix_g, ffn2_g = as_rows(ffn1_norm), as_rows(mix_norm), as_rows(ffn2_norm)
    fox_qg, fox_kg = as_rows(fox_q_norm), as_rows(fox_k_norm)
    ret_g = ret_norm.reshape(depth * ret_heads, 1, ret_dh)
    fg_bias = as_rows(jnp.pad(fox_f_bias, ((0, 0), (0, V7X_LANES - fox_heads))))
    zeros = jnp.zeros_like(lru_lambda)
    lru_prm = jnp.stack([lru_conv_b, lru_b_a, lru_b_x, lru_lambda, zeros, zeros, zeros, zeros], axis=1)

    w_in_t = jnp.swapaxes(w_in, 1, 2)
    assert c_fg % 8 == 0 and c_tail % 8 == 0 and c_fg + V7X_LANES <= w_in.shape[-1]

    half = ret_dh // 2
    inv_freq = ROPE_BASE ** (-jnp.arange(half, dtype=f32) / half)
    ang = jnp.arange(seq).astype(f32)[:, None] * inv_freq[None, :]
    cos, sin = jnp.cos(ang), jnp.sin(ang)

    x = x.reshape(batch * seq, d)
    gu_cfg = {(0, 1): (512, 512, True), (0, 2): (1024, 256, False), (1, 1): (1024, 256, True), (1, 2): (512, 512, False)}
    dn_cfg = {(0, 1): (512, 512, True), (0, 2): (1024, 256, False), (1, 1): (512, 512, False), (1, 2): (1024, 256, True)}
    for l in range(depth):
        lc = min(l, 1)
        h = _gateup(_rmsnorm(x, ffn1_g, l), ffn1_w_gate, ffn1_w_up, l, *gu_cfg[lc, 1])
        x = _mm_resid(h, ffn1_w_down, x, l, HALF_STEP, "ffn_down", *dn_cfg[lc, 1])

        hn = _rmsnorm(x, mix_g, l)
        proj = functools.partial(_mm_nt, hn, w_in_t, layer=l, weights_outer=(lc == 0))
        p_ret = proj(row0=0, n=4 * ret_w, out_dtype=ACT_DTYPE, act=None, name="proj_ret")
        p_fox = proj(row0=c_fox, n=3 * fox_w, out_dtype=ACT_DTYPE, act=None, name="proj_fox")
        fg = proj(row0=c_fg, n=V7X_LANES, out_dtype=f32, act=None, name="proj_fgate")
        p_lru = proj(row0=c_tail, n=2 * lru_w, out_dtype=ACT_DTYPE, act=None, name="proj_lru")
        gates = proj(row0=c_tail + 2 * lru_w, n=N_BRANCH * d, out_dtype=ACT_DTYPE, act="sigmoid", name="proj_gates")

        ret = _retention(p_ret, cos, sin, ret_g, l, batch, seq, ret_heads)
        cum = _logf_cumsum(fg, fg_bias, l, batch, seq)[:, :fox_heads].reshape(batch * fox_heads, seq)
        fox = _fox(p_fox, cum, fox_qg, fox_kg, l, batch, seq, fox_heads)
        lru = _lru(p_lru, lru_conv_w, lru_prm, lru_w_a, lru_w_x, l, batch, seq)

        mixed = _merge(ret, fox, lru, w_branch_ret, w_branch_fox, w_branch_lru, gates, l)
        x = _mm_resid(mixed, w_out, x, l, 1.0, "out_proj", weights_outer=(lc == 0))

        h = _gateup(_rmsnorm(x, ffn2_g, l), ffn2_w_gate, ffn2_w_up, l, *gu_cfg[lc, 2])
        x = _mm_resid(h, ffn2_w_down, x, l, HALF_STEP, "ffn_down", *dn_cfg[lc, 2])
    return x.reshape(batch, seq, d)
```

```python
import functools
import math

import jax
import jax.numpy as jnp
from jax import lax
from jax.experimental import pallas as pl
from jax.experimental.pallas import tpu as pltpu

EPS = 1e-6
HALF_STEP = 0.5
ROPE_BASE = 10000.0
LRU_C = 8.0
N_BRANCH = 3
LOG2_E = 1.4426950408889634

V7X_LANES = 128
V7X_VMEM_REQUEST_CAP = 62 * 1024 * 1024
ACT_DTYPE = jnp.bfloat16


def _vmem_limit(pipelined_bytes, resident_bytes=0):
    need = 2 * pipelined_bytes + resident_bytes
    return int(min(max(need, 16 * 1024 * 1024), V7X_VMEM_REQUEST_CAP))


def _nbytes(shape, dtype):
    return math.prod(shape) * jnp.dtype(dtype).itemsize


def _tile(n, pref):
    t = min(n, pref)
    assert n % t == 0, (n, pref)
    return t


def _dot(a, b, dims=(((1,), (0,)), ((), ()))):
    return lax.dot_general(a, b, dims, preferred_element_type=jnp.float32)


_NT = (((1,), (1,)), ((), ()))
_TN = (((0,), (0,)), ((), ()))


def _sigmoid(x):
    return 1.0 / (1.0 + jnp.exp(-x))


def _softplus(x):
    return jnp.maximum(x, 0.0) + jnp.log1p(jnp.exp(-jnp.abs(x)))


def _lane_partial(x):
    parts = [x[:, c:c + V7X_LANES] for c in range(0, x.shape[-1], V7X_LANES)]
    return functools.reduce(lambda p, q: p + q, parts)


def _row_scale(r_ref, width):
    r = r_ref[...]
    return r if width == V7X_LANES else jnp.concatenate([r] * (width // V7X_LANES), axis=-1)


def _emit_norm(x_new, gain_ref, xg_ref, r_ref, j, nj, d):
    xg_ref[...] = (x_new * gain_ref[...]).astype(xg_ref.dtype)
    part = _lane_partial(x_new * x_new)

    @pl.when(j == 0)
    def _():
        r_ref[...] = part

    @pl.when(j > 0)
    def _():
        r_ref[...] += part

    @pl.when(j == nj - 1)
    def _():
        ssq = jnp.sum(r_ref[...], axis=-1, keepdims=True)
        r_ref[...] = jnp.broadcast_to(lax.rsqrt(ssq * (1.0 / d) + EPS), r_ref.shape)


def _prep_kernel(x_ref, g_ref, xg_ref, r_ref):
    x = x_ref[...]
    xg_ref[...] = (x * g_ref[...]).astype(xg_ref.dtype)
    ms = jnp.mean(x * x, axis=-1, keepdims=True)
    r_ref[...] = jnp.broadcast_to(lax.rsqrt(ms + EPS), r_ref.shape)


def _prep(x, gains, layer):
    m, d = x.shape
    tr = _tile(m, 256)
    return pl.pallas_call(
        _prep_kernel,
        grid=(m // tr,),
        in_specs=[pl.BlockSpec((tr, d), lambda i: (i, 0)),
                  pl.BlockSpec((None, 1, d), lambda i: (layer, 0, 0))],
        out_specs=[pl.BlockSpec((tr, d), lambda i: (i, 0)),
                   pl.BlockSpec((tr, V7X_LANES), lambda i: (i, 0))],
        out_shape=[jax.ShapeDtypeStruct((m, d), ACT_DTYPE),
                   jax.ShapeDtypeStruct((m, V7X_LANES), jnp.float32)],
        compiler_params=pltpu.CompilerParams(
            dimension_semantics=("arbitrary",),
            vmem_limit_bytes=_vmem_limit(_nbytes((tr, d), jnp.float32) + _nbytes((tr, d), ACT_DTYPE),
                                         2 * _nbytes((tr, d), jnp.float32))),
        name="input_norm",
    )(x, gains)


def _gateup_kernel(a_ref, r_ref, wg_ref, wu_ref, o_ref):
    a = a_ref[...]
    r = _row_scale(r_ref, o_ref.shape[-1])
    g = r * _dot(a, wg_ref[...])
    u = r * _dot(a, wu_ref[...])
    o_ref[...] = (g * _sigmoid(g) * u).astype(o_ref.dtype)


def _gateup(xg, r, w_gate, w_up, layer):
    m, k = xg.shape
    f = w_gate.shape[-1]
    tm, tf = _tile(m, 2048), _tile(f, 256)
    w_spec = pl.BlockSpec((None, k, tf), lambda i, j: (layer, 0, j))
    return pl.pallas_call(
        _gateup_kernel,
        grid=(m // tm, f // tf),
        in_specs=[pl.BlockSpec((tm, k), lambda i, j: (i, 0)),
                  pl.BlockSpec((tm, V7X_LANES), lambda i, j: (i, 0)), w_spec, w_spec],
        out_specs=pl.BlockSpec((tm, tf), lambda i, j: (i, j)),
        out_shape=jax.ShapeDtypeStruct((m, f), ACT_DTYPE),
        compiler_params=pltpu.CompilerParams(
            dimension_semantics=("arbitrary", "arbitrary"),
            vmem_limit_bytes=_vmem_limit(
                _nbytes((tm, k), xg.dtype) + 2 * _nbytes((k, tf), w_gate.dtype) + _nbytes((tm, tf), ACT_DTYPE),
                4 * _nbytes((tm, tf), jnp.float32))),
        name="ffn_gateup",
    )(xg, r, w_gate, w_up)


def _mm_resid_kernel(a_ref, w_ref, x_ref, *rest, scale, nj, d, emit):
    x_new = x_ref[...] + scale * _dot(a_ref[...], w_ref[...])
    if emit:
        gain_ref, o_ref, xg_ref, r_ref = rest
        o_ref[...] = x_new
        _emit_norm(x_new, gain_ref, xg_ref, r_ref, pl.program_id(1), nj, d)
    else:
        (o_ref,) = rest
        o_ref[...] = x_new


def _mm_resid(a, w, x, layer, scale, name, next_gain=None, tn=512):
    m, k = a.shape
    n = w.shape[-1]
    tm, tn = _tile(m, 1024), _tile(n, tn)
    nj = n // tn
    emit = next_gain is not None
    tile = pl.BlockSpec((tm, tn), lambda i, j: (i, j))
    in_specs = [pl.BlockSpec((tm, k), lambda i, j: (i, 0)),
                pl.BlockSpec((None, k, tn), lambda i, j: (layer, 0, j)), tile]
    out_specs, out_shape, args = tile, jax.ShapeDtypeStruct((m, n), jnp.float32), [a, w, x]
    if emit:
        gains, gl = next_gain
        in_specs.append(pl.BlockSpec((None, 1, tn), lambda i, j: (gl, 0, j)))
        args.append(gains)
        out_specs = [tile, tile, pl.BlockSpec((tm, V7X_LANES), lambda i, j: (i, 0))]
        out_shape = [out_shape, jax.ShapeDtypeStruct((m, n), ACT_DTYPE),
                     jax.ShapeDtypeStruct((m, V7X_LANES), jnp.float32)]
    return pl.pallas_call(
        functools.partial(_mm_resid_kernel, scale=scale, nj=nj, d=n, emit=emit),
        grid=(m // tm, nj),
        in_specs=in_specs, out_specs=out_specs, out_shape=out_shape,
        compiler_params=pltpu.CompilerParams(
            dimension_semantics=("arbitrary", "arbitrary"),
            vmem_limit_bytes=_vmem_limit(
                _nbytes((tm, k), a.dtype) + _nbytes((k, tn), w.dtype) + 3 * _nbytes((tm, tn), jnp.float32),
                2 * _nbytes((tm, tn), jnp.float32))),
        name=name,
    )(*args)


def _mm_nt_kernel(a_ref, r_ref, wt_ref, o_ref, *, act):
    y = _row_scale(r_ref, o_ref.shape[-1]) * _dot(a_ref[...], wt_ref[0], _NT)
    if act == "sigmoid":
        y = _sigmoid(y)
    o_ref[...] = y.astype(o_ref.dtype)


def _mm_nt(xg, r, wt, *, layer, row0, n, out_dtype, act, name):
    m, k = xg.shape
    tm, tn = _tile(m, 2048), _tile(n, 512)
    return pl.pallas_call(
        functools.partial(_mm_nt_kernel, act=act),
        grid=(m // tm, n // tn),
        in_specs=[pl.BlockSpec((tm, k), lambda i, j: (i, 0)),
                  pl.BlockSpec((tm, V7X_LANES), lambda i, j: (i, 0)),
                  pl.BlockSpec((pl.Element(1), pl.Element(tn), pl.Element(k)),
                               lambda i, j: (layer, pl.multiple_of(row0 + j * tn, 8), 0))],
        out_specs=pl.BlockSpec((tm, tn), lambda i, j: (i, j)),
        out_shape=jax.ShapeDtypeStruct((m, n), out_dtype),
        compiler_params=pltpu.CompilerParams(
            dimension_semantics=("arbitrary", "arbitrary"),
            vmem_limit_bytes=_vmem_limit(
                _nbytes((tm, k), xg.dtype) + _nbytes((tn, k), wt.dtype) + _nbytes((tm, tn), out_dtype),
                2 * _nbytes((tm, tn), jnp.float32))),
        name=name,
    )(xg, r, wt)


def _merge_kernel(r_ref, f_ref, l_ref, wr_ref, wf_ref, wl_ref, gr_ref, gf_ref, gl_ref, o_ref):
    y = gr_ref[...].astype(jnp.float32) * _dot(r_ref[...], wr_ref[...])
    y = y + gf_ref[...].astype(jnp.float32) * _dot(f_ref[...], wf_ref[...])
    y = y + gl_ref[...].astype(jnp.float32) * _dot(l_ref[...], wl_ref[...])
    o_ref[...] = y.astype(o_ref.dtype)


def _merge(ret, fox, lru, w_ret, w_fox, w_lru, gates, layer):
    m, kw = ret.shape
    d = w_ret.shape[-1]
    tm, tn = _tile(m, 1024), _tile(d, 512)
    nj = d // tn
    a_spec = pl.BlockSpec((tm, kw), lambda j, i: (i, 0))
    w_spec = pl.BlockSpec((None, kw, tn), lambda j, i: (layer, 0, j))
    g_specs = [pl.BlockSpec((tm, tn), functools.partial(lambda j, i, o: (i, o + j), o=b * nj))
               for b in range(N_BRANCH)]
    return pl.pallas_call(
        _merge_kernel,
        grid=(nj, m // tm),
        in_specs=[a_spec, a_spec, a_spec, w_spec, w_spec, w_spec] + g_specs,
        out_specs=pl.BlockSpec((tm, tn), lambda j, i: (i, j)),
        out_shape=jax.ShapeDtypeStruct((m, d), ACT_DTYPE),
        compiler_params=pltpu.CompilerParams(
            dimension_semantics=("arbitrary", "arbitrary"),
            vmem_limit_bytes=_vmem_limit(
                3 * _nbytes((tm, kw), ret.dtype) + 3 * _nbytes((kw, tn), w_ret.dtype) + 4 * _nbytes((tm, tn), ACT_DTYPE),
                4 * _nbytes((tm, tn), jnp.float32))),
        name="branch_merge",
    )(ret, fox, lru, w_ret, w_fox, w_lru, gates, gates, gates)


def _retention_kernel(q_ref, k_ref, v_ref, g_ref, cos_ref, sin_ref, gain_ref, o_ref, *, chunk, n_chunks):
    dh = q_ref.shape[-1]
    half = dh // 2
    f32 = jnp.float32
    h = pl.program_id(1).astype(f32)
    log_gamma = jnp.log(1.0 - jnp.exp2(jnp.full((1, 1), -5.0, f32) - h))
    row = lax.broadcasted_iota(jnp.int32, (chunk, chunk), 0)
    col = lax.broadcasted_iota(jnp.int32, (chunk, chunk), 1)
    rel = (row - col).astype(f32)
    inner_decay = jnp.where(rel >= 0, jnp.exp(jnp.maximum(rel, 0.0) * log_gamma), 0.0)
    idx = lax.broadcasted_iota(jnp.int32, (chunk, 1), 0).astype(f32)
    k_to_end = jnp.exp((chunk - 1.0 - idx) * log_gamma)
    q_from_start = jnp.exp((idx + 1.0) * log_gamma)
    chunk_decay = jnp.exp(chunk * log_gamma)
    gain = gain_ref[...]

    def rotary(x, cos, sin):
        x1, x2 = x[:, :half], x[:, half:]
        return jnp.concatenate([x1 * cos - x2 * sin, x2 * cos + x1 * sin], axis=-1)

    def body(n, state):
        rows = pl.ds(pl.multiple_of(n * chunk, chunk), chunk)
        cos, sin = cos_ref[rows, :], sin_ref[rows, :]
        q = rotary(q_ref[rows, :].astype(f32), cos, sin)
        k = rotary(k_ref[rows, :].astype(f32), cos, sin) * dh ** -0.5
        v = v_ref[rows, :]
        scores = _dot(q.astype(ACT_DTYPE), k.astype(ACT_DTYPE), _NT) * inner_decay
        inner = _dot(scores.astype(ACT_DTYPE), v)
        cross = _dot((q * q_from_start).astype(ACT_DTYPE), state.astype(ACT_DTYPE))
        kv = _dot((k * k_to_end).astype(ACT_DTYPE), v, _TN)
        o = inner + cross
        o = o * lax.rsqrt(jnp.mean(o * o, axis=-1, keepdims=True) + EPS) * gain
        g = g_ref[rows, :].astype(f32)
        o_ref[rows, :] = (g * _sigmoid(g) * o).astype(o_ref.dtype)
        return chunk_decay * state + kv

    lax.fori_loop(0, n_chunks, body, jnp.zeros((dh, dh), f32))


def _retention(p_ret, cos, sin, gains, layer, batch, seq, heads):
    m, w4 = p_ret.shape
    dh = w4 // (4 * heads)
    chunk = _tile(seq, 512)
    blk = lambda seg: pl.BlockSpec((seq, dh), lambda b, h: (b, seg * heads + h))
    tab = pl.BlockSpec((seq, dh // 2), lambda b, h: (0, 0))
    return pl.pallas_call(
        functools.partial(_retention_kernel, chunk=chunk, n_chunks=seq // chunk),
        grid=(batch, heads),
        in_specs=[blk(0), blk(1), blk(2), blk(3), tab, tab,
                  pl.BlockSpec((None, 1, dh), lambda b, h: (layer * heads + h, 0, 0))],
        out_specs=pl.BlockSpec((seq, dh), lambda b, h: (b, h)),
        out_shape=jax.ShapeDtypeStruct((m, heads * dh), ACT_DTYPE),
        compiler_params=pltpu.CompilerParams(
            dimension_semantics=("arbitrary", "arbitrary"),
            vmem_limit_bytes=_vmem_limit(
                5 * _nbytes((seq, dh), p_ret.dtype) + 2 * _nbytes((seq, dh // 2), jnp.float32),
                12 * _nbytes((chunk, max(chunk, dh)), jnp.float32))),
        name="retention",
    )(p_ret, p_ret, p_ret, p_ret, cos, sin, gains)


def _logf_cumsum_kernel(fg_ref, bias_ref, o_ref, *, chunk, n_chunks):
    f32 = jnp.float32
    row = lax.broadcasted_iota(jnp.int32, (chunk, chunk), 0)
    col = lax.broadcasted_iota(jnp.int32, (chunk, chunk), 1)
    upper = (row <= col).astype(ACT_DTYPE)
    bias = bias_ref[...]

    def body(n, carry):
        rows = pl.ds(pl.multiple_of(n * chunk, chunk), chunk)
        log_f = -_softplus(-(fg_ref[rows, :] + bias))
        p1 = log_f.astype(ACT_DTYPE)
        r1 = log_f - p1.astype(f32)
        p2 = r1.astype(ACT_DTYPE)
        p3 = (r1 - p2.astype(f32)).astype(ACT_DTYPE)
        cs = _dot(p1, upper, _TN) + _dot(p2, upper, _TN) + _dot(p3, upper, _TN)
        cs = cs + carry
        o_ref[:, rows] = cs
        return cs[:, chunk - 1:chunk]

    lax.fori_loop(0, n_chunks, body, jnp.zeros((fg_ref.shape[-1], 1), f32))


def _logf_cumsum(fg, bias, layer, batch, seq):
    lanes = fg.shape[-1]
    chunk = _tile(seq, 512)
    return pl.pallas_call(
        functools.partial(_logf_cumsum_kernel, chunk=chunk, n_chunks=seq // chunk),
        grid=(batch,),
        in_specs=[pl.BlockSpec((seq, lanes), lambda b: (b, 0)),
                  pl.BlockSpec((None, 1, lanes), lambda b: (layer, 0, 0))],
        out_specs=pl.BlockSpec((None, lanes, seq), lambda b: (b, 0, 0)),
        out_shape=jax.ShapeDtypeStruct((batch, lanes, seq), jnp.float32),
        compiler_params=pltpu.CompilerParams(
            dimension_semantics=("arbitrary",),
            vmem_limit_bytes=_vmem_limit(2 * _nbytes((seq, lanes), jnp.float32),
                                         8 * _nbytes((chunk, chunk), jnp.float32))),
        name="fox_logf_cumsum",
    )(fg, bias)


def _head_rms(x, gain):
    return x * lax.rsqrt(jnp.mean(x * x, axis=-1, keepdims=True) + EPS) * gain


FOX_HEADS_PER_STEP = 2


def _fox_kernel(q_ref, k_ref, v_ref, cum_ref, qg_ref, kg_ref, o_ref, kn_ref, *, tq, nq, dh, scale):
    f32 = jnp.float32
    qi = pl.program_id(2)
    head_cols = [slice(hh * dh, (hh + 1) * dh) for hh in range(q_ref.shape[-1] // dh)]

    @pl.when(qi == 0)
    def _():
        for cs in head_cols:
            kn_ref[:, cs] = (_head_rms(k_ref[:, cs].astype(f32), kg_ref[...]) * (scale * LOG2_E)).astype(kn_ref.dtype)

    causal = (lax.broadcasted_iota(jnp.int32, (tq, tq), 0) >= lax.broadcasted_iota(jnp.int32, (tq, tq), 1))

    def one_head(hh, cs, lo):
        q = _head_rms(q_ref[:, cs].astype(f32), qg_ref[...]).astype(ACT_DTYPE)

        def logits(a, b):
            return _dot(q, kn_ref[a:b, cs], _NT) - cum_ref[hh:hh + 1, a:b] * LOG2_E

        s_diag = jnp.where(causal, logits(lo, lo + tq), -jnp.inf)
        m = jnp.max(s_diag, axis=-1, keepdims=True)
        if lo:
            s_past = logits(0, lo)
            m = jnp.maximum(m, jnp.max(s_past, axis=-1, keepdims=True))
        p = jnp.exp2(s_diag - m)
        l = jnp.sum(p, axis=-1, keepdims=True)
        acc = _dot(p.astype(ACT_DTYPE), v_ref[lo:lo + tq, cs])
        if lo:
            p = jnp.exp2(s_past - m)
            l = l + jnp.sum(p, axis=-1, keepdims=True)
            acc = acc + _dot(p.astype(ACT_DTYPE), v_ref[0:lo, cs])
        o_ref[:, cs] = (acc / l).astype(o_ref.dtype)

    for case in range(nq):
        @pl.when(qi == case)
        def _(lo=case * tq):
            for hh, cs in enumerate(head_cols):
                one_head(hh, cs, lo)


def _fox(p_fox, cum, q_gain, k_gain, layer, batch, seq, heads):
    m, w3 = p_fox.shape
    dh = w3 // (3 * heads)
    hp = FOX_HEADS_PER_STEP if heads % FOX_HEADS_PER_STEP == 0 else 1
    hg = heads // hp
    tq = _tile(seq, 512)
    nq = seq // tq
    cum = cum.reshape(batch * hg, hp, seq)
    gain = pl.BlockSpec((None, 1, dh), lambda b, g, i: (layer, 0, 0))
    return pl.pallas_call(
        functools.partial(_fox_kernel, tq=tq, nq=nq, dh=dh, scale=dh ** -0.5),
        grid=(batch, hg, nq),
        in_specs=[pl.BlockSpec((tq, hp * dh), lambda b, g, i: (b * nq + i, g)),
                  pl.BlockSpec((seq, hp * dh), lambda b, g, i: (b, hg + g)),
                  pl.BlockSpec((seq, hp * dh), lambda b, g, i: (b, 2 * hg + g)),
                  pl.BlockSpec((None, hp, seq), lambda b, g, i: (b * hg + g, 0, 0)),
                  gain, gain],
        out_specs=pl.BlockSpec((tq, hp * dh), lambda b, g, i: (b * nq + i, g)),
        out_shape=jax.ShapeDtypeStruct((m, heads * dh), ACT_DTYPE),
        scratch_shapes=[pltpu.VMEM((seq, hp * dh), ACT_DTYPE)],
        compiler_params=pltpu.CompilerParams(
            dimension_semantics=("arbitrary", "arbitrary", "arbitrary"),
            vmem_limit_bytes=_vmem_limit(3 * _nbytes((seq, hp * dh), p_fox.dtype),
                                         _nbytes((seq, hp * dh), ACT_DTYPE) + 4 * hp * _nbytes((tq, seq), jnp.float32))),
        name="forgetting_attention",
    )(p_fox, p_fox, p_fox, cum, q_gain, k_gain)


LRU_PAD_ROWS = 8


def _lru_kernel(gate_ref, x_ref, cw_ref, prm_ref, wa_ref, wx_ref, o_ref, xs_ref, us_ref, as_ref, *, conv_width):
    f32 = jnp.float32
    seq, width = x_ref.shape
    pad = LRU_PAD_ROWS
    assert conv_width - 1 <= pad
    body = pl.ds(pad, seq)

    def shifted(ref, s):
        return ref[pl.ds(pad - s, seq), :]

    x = x_ref[...].astype(f32)
    xs_ref[0:pad, :] = jnp.zeros((pad, width), f32)
    xs_ref[body, :] = x
    conv_b, b_a, b_x, lam = (prm_ref[i:i + 1, :] for i in range(4))
    xc = conv_b + x * cw_ref[conv_width - 1:conv_width, :]
    for s in range(1, conv_width):
        xc = xc + shifted(xs_ref, s) * cw_ref[conv_width - 1 - s:conv_width - s, :]
    xcb = xc.astype(ACT_DTYPE)
    r = _sigmoid(_dot(xcb, wa_ref[...]) + b_a)
    i = _sigmoid(_dot(xcb, wx_ref[...]) + b_x)
    log_a = -LRU_C * r * _softplus(-lam)
    a = jnp.exp(log_a)
    u = jnp.sqrt(-jnp.tanh(log_a) * (a * a + 1.0)) * (i * xc)
    us_ref[0:pad, :] = jnp.zeros((pad, width), f32)
    as_ref[0:pad, :] = jnp.ones((pad, width), f32)
    s = 1
    while s < seq:
        last = 2 * s >= seq
        if s < pad:
            us_ref[body, :] = u
            as_ref[body, :] = a
            u = a * shifted(us_ref, s) + u
            a = a if last else a * shifted(as_ref, s)
        else:
            u = jnp.concatenate([u[:s], a[s:] * u[:-s] + u[s:]], axis=0)
            a = a if last else jnp.concatenate([a[:s], a[s:] * a[:-s]], axis=0)
        s *= 2
    o_ref[...] = (jax.nn.gelu(gate_ref[...].astype(f32)) * u).astype(o_ref.dtype)


def _lru(p_lru, conv_w, prm, w_a, w_x, layer, batch, seq):
    m, w2 = p_lru.shape
    nblk, bw = w_a.shape[1], w_a.shape[2]
    conv_width = conv_w.shape[1]
    wmat = pl.BlockSpec((None, None, bw, bw), lambda b, c: (layer, c, 0, 0))
    return pl.pallas_call(
        functools.partial(_lru_kernel, conv_width=conv_width),
        grid=(batch, nblk),
        in_specs=[pl.BlockSpec((seq, bw), lambda b, c: (b, c)),
                  pl.BlockSpec((seq, bw), lambda b, c: (b, nblk + c)),
                  pl.BlockSpec((None, conv_width, bw), lambda b, c: (layer, 0, c)),
                  pl.BlockSpec((None, 8, bw), lambda b, c: (layer, 0, c)),
                  wmat, wmat],
        out_specs=pl.BlockSpec((seq, bw), lambda b, c: (b, c)),
        out_shape=jax.ShapeDtypeStruct((m, w2 // 2), ACT_DTYPE),
        scratch_shapes=[pltpu.VMEM((LRU_PAD_ROWS + seq, bw), jnp.float32)] * 3,
        compiler_params=pltpu.CompilerParams(
            dimension_semantics=("arbitrary", "arbitrary"),
            vmem_limit_bytes=_vmem_limit(3 * _nbytes((seq, bw), p_lru.dtype),
                                         16 * _nbytes((seq, bw), jnp.float32))),
        name="rg_lru",
    )(p_lru, p_lru, conv_w, prm, w_a, w_x)


def kernel(x, ffn1_norm, ffn1_w_gate, ffn1_w_up, ffn1_w_down, mix_norm, w_in, ret_norm, fox_q_norm, fox_k_norm, fox_f_bias, lru_conv_w, lru_conv_b, lru_w_a, lru_b_a, lru_w_x, lru_b_x, lru_lambda, w_branch_ret, w_branch_fox, w_branch_lru, w_out, ffn2_norm, ffn2_w_gate, ffn2_w_up, ffn2_w_down):
    batch, seq, d = x.shape
    depth = w_in.shape[0]
    ret_heads, ret_dh = ret_norm.shape[1], ret_norm.shape[2]
    fox_heads, fox_dh = fox_f_bias.shape[1], fox_q_norm.shape[1]
    ret_w, fox_w, lru_w = ret_heads * ret_dh, fox_heads * fox_dh, lru_lambda.shape[1]
    assert fox_heads <= V7X_LANES
    c_fox = 4 * ret_w
    c_fg = c_fox + 3 * fox_w
    c_tail = c_fg + fox_heads
    assert w_in.shape[-1] == c_tail + 2 * lru_w + N_BRANCH * d

    f32 = jnp.float32
    as_rows = lambda p: p.reshape(depth, 1, -1)
    ffn1_g, mix_g, ffn2_g = as_rows(ffn1_norm), as_rows(mix_norm), as_rows(ffn2_norm)
    fox_qg, fox_kg = as_rows(fox_q_norm), as_rows(fox_k_norm)
    ret_g = ret_norm.reshape(depth * ret_heads, 1, ret_dh)
    fg_bias = as_rows(jnp.pad(fox_f_bias, ((0, 0), (0, V7X_LANES - fox_heads))))
    zeros = jnp.zeros_like(lru_lambda)
    lru_prm = jnp.stack([lru_conv_b, lru_b_a, lru_b_x, lru_lambda, zeros, zeros, zeros, zeros], axis=1)

    w_in_t = jnp.swapaxes(w_in, 1, 2)
    assert c_fg % 8 == 0 and c_tail % 8 == 0 and c_fg + V7X_LANES <= w_in.shape[-1]

    half = ret_dh // 2
    inv_freq = ROPE_BASE ** (-jnp.arange(half, dtype=f32) / half)
    ang = jnp.arange(seq).astype(f32)[:, None] * inv_freq[None, :]
    cos, sin = jnp.cos(ang), jnp.sin(ang)

    x = x.reshape(batch * seq, d)
    xg, r = _prep(x, ffn1_g, 0)
    for l in range(depth):
        h = _gateup(xg, r, ffn1_w_gate, ffn1_w_up, l)
        x, xg, r = _mm_resid(h, ffn1_w_down, x, l, HALF_STEP, "ffn_down", next_gain=(mix_g, l), tn=256)

        proj = functools.partial(_mm_nt, xg, r, w_in_t, layer=l)
        p_ret = proj(row0=0, n=4 * ret_w, out_dtype=ACT_DTYPE, act=None, name="proj_ret")
        p_fox = proj(row0=c_fox, n=3 * fox_w, out_dtype=ACT_DTYPE, act=None, name="proj_fox")
        fg = proj(row0=c_fg, n=V7X_LANES, out_dtype=f32, act=None, name="proj_fgate")
        p_lru = proj(row0=c_tail, n=2 * lru_w, out_dtype=ACT_DTYPE, act=None, name="proj_lru")
        gates = proj(row0=c_tail + 2 * lru_w, n=N_BRANCH * d, out_dtype=ACT_DTYPE, act="sigmoid", name="proj_gates")

        ret = _retention(p_ret, cos, sin, ret_g, l, batch, seq, ret_heads)
        cum = _logf_cumsum(fg, fg_bias, l, batch, seq)[:, :fox_heads].reshape(batch * fox_heads, seq)
        fox = _fox(p_fox, cum, fox_qg, fox_kg, l, batch, seq, fox_heads)
        lru = _lru(p_lru, lru_conv_w, lru_prm, lru_w_a, lru_w_x, l, batch, seq)

        mixed = _merge(ret, fox, lru, w_branch_ret, w_branch_fox, w_branch_lru, gates, l)
        x, xg, r = _mm_resid(mixed, w_out, x, l, 1.0, "out_proj", next_gain=(ffn2_g, l))

        h = _gateup(xg, r, ffn2_w_gate, ffn2_w_up, l)
        if l + 1 < depth:
            x, xg, r = _mm_resid(h, ffn2_w_down, x, l, HALF_STEP, "ffn_down", next_gain=(ffn1_g, l + 1), tn=256)
        else:
            x = _mm_resid(h, ffn2_w_down, x, l, HALF_STEP, "ffn_down", tn=256)
    return x.reshape(batch, seq, d)
```

```python
import functools
import math

import jax
import jax.numpy as jnp
from jax import lax
from jax.experimental import pallas as pl
from jax.experimental.pallas import tpu as pltpu

EPS = 1e-6
HALF_STEP = 0.5
ROPE_BASE = 10000.0
LRU_C = 8.0
N_BRANCH = 3
LOG2_E = 1.4426950408889634

V7X_LANES = 128
V7X_VMEM_REQUEST_CAP = 62 * 1024 * 1024
ACT_DTYPE = jnp.bfloat16


def _vmem_limit(pipelined_bytes, resident_bytes=0):
    need = 2 * pipelined_bytes + resident_bytes
    return int(min(max(need, 16 * 1024 * 1024), V7X_VMEM_REQUEST_CAP))


def _nbytes(shape, dtype):
    return math.prod(shape) * jnp.dtype(dtype).itemsize


def _tile(n, pref):
    t = min(n, pref)
    assert n % t == 0, (n, pref)
    return t


def _dot(a, b, dims=(((1,), (0,)), ((), ()))):
    return lax.dot_general(a, b, dims, preferred_element_type=jnp.float32)


_NT = (((1,), (1,)), ((), ()))
_TN = (((0,), (0,)), ((), ()))


def _sigmoid(x):
    return 1.0 / (1.0 + jnp.exp(-x))


def _softplus(x):
    return jnp.maximum(x, 0.0) + jnp.log1p(jnp.exp(-jnp.abs(x)))


def _lane_partial(x):
    parts = [x[:, c:c + V7X_LANES] for c in range(0, x.shape[-1], V7X_LANES)]
    return functools.reduce(lambda p, q: p + q, parts)


def _row_scale(r_ref, width):
    r = r_ref[...]
    return r if width == V7X_LANES else jnp.concatenate([r] * (width // V7X_LANES), axis=-1)


def _emit_norm(x_new, gain_ref, xg_ref, r_ref, j, nj, d):
    xg_ref[...] = (x_new * gain_ref[...]).astype(xg_ref.dtype)
    r_ref[...] += _lane_partial(x_new * x_new)

    @pl.when(j == nj - 1)
    def _():
        ssq = jnp.sum(r_ref[...], axis=-1, keepdims=True)
        r_ref[...] = jnp.broadcast_to(lax.rsqrt(ssq * (1.0 / d) + EPS), r_ref.shape)


def _prep_kernel(x_ref, g_ref, xg_ref, r_ref):
    x = x_ref[...]
    xg_ref[...] = (x * g_ref[...]).astype(xg_ref.dtype)
    ms = jnp.mean(x * x, axis=-1, keepdims=True)
    r_ref[...] = jnp.broadcast_to(lax.rsqrt(ms + EPS), r_ref.shape)


def _prep(x, gains, layer):
    m, d = x.shape
    tr = _tile(m, 256)
    return pl.pallas_call(
        _prep_kernel,
        grid=(m // tr,),
        in_specs=[pl.BlockSpec((tr, d), lambda i: (i, 0)),
                  pl.BlockSpec((None, 1, d), lambda i: (layer, 0, 0))],
        out_specs=[pl.BlockSpec((tr, d), lambda i: (i, 0)),
                   pl.BlockSpec((tr, V7X_LANES), lambda i: (i, 0))],
        out_shape=[jax.ShapeDtypeStruct((m, d), ACT_DTYPE),
                   jax.ShapeDtypeStruct((m, V7X_LANES), jnp.float32)],
        compiler_params=pltpu.CompilerParams(
            dimension_semantics=("arbitrary",),
            vmem_limit_bytes=_vmem_limit(_nbytes((tr, d), jnp.float32) + _nbytes((tr, d), ACT_DTYPE),
                                         2 * _nbytes((tr, d), jnp.float32))),
        name="input_norm",
    )(x, gains)


def _gateup_kernel(a_ref, r_ref, wg_ref, wu_ref, o_ref):
    a = a_ref[...]
    r = _row_scale(r_ref, o_ref.shape[-1])
    g = r * _dot(a, wg_ref[...])
    u = r * _dot(a, wu_ref[...])
    o_ref[...] = (g * _sigmoid(g) * u).astype(o_ref.dtype)


def _gateup(xg, r, w_gate, w_up, layer):
    m, k = xg.shape
    f = w_gate.shape[-1]
    tm, tf = _tile(m, 2048), _tile(f, 256)
    w_spec = pl.BlockSpec((None, k, tf), lambda i, j: (layer, 0, j))
    return pl.pallas_call(
        _gateup_kernel,
        grid=(m // tm, f // tf),
        in_specs=[pl.BlockSpec((tm, k), lambda i, j: (i, 0)),
                  pl.BlockSpec((tm, V7X_LANES), lambda i, j: (i, 0)), w_spec, w_spec],
        out_specs=pl.BlockSpec((tm, tf), lambda i, j: (i, j)),
        out_shape=jax.ShapeDtypeStruct((m, f), ACT_DTYPE),
        compiler_params=pltpu.CompilerParams(
            dimension_semantics=("arbitrary", "arbitrary"),
            vmem_limit_bytes=_vmem_limit(
                _nbytes((tm, k), xg.dtype) + 2 * _nbytes((k, tf), w_gate.dtype) + _nbytes((tm, tf), ACT_DTYPE),
                4 * _nbytes((tm, tf), jnp.float32))),
        name="ffn_gateup",
    )(xg, r, w_gate, w_up)


def _mm_resid_kernel(a_ref, w_ref, x_ref, *rest, scale, nj, d, emit):
    if emit:
        gain_ref, o_ref, xg_ref, r_ref = rest
        j = pl.program_id(1)

        @pl.when(j == 0)
        def _():
            r_ref[...] = jnp.zeros(r_ref.shape, r_ref.dtype)
    else:
        (o_ref,) = rest
    x_new = x_ref[...] + scale * _dot(a_ref[...], w_ref[...])
    o_ref[...] = x_new
    if emit:
        _emit_norm(x_new, gain_ref, xg_ref, r_ref, j, nj, d)


def _mm_resid(a, w, x, layer, scale, name, next_gain=None):
    m, k = a.shape
    n = w.shape[-1]
    tm, tn = _tile(m, 1024), _tile(n, 512 if k <= 4096 else 256)
    nj = n // tn
    emit = next_gain is not None
    tile = pl.BlockSpec((tm, tn), lambda i, j: (i, j))
    in_specs = [pl.BlockSpec((tm, k), lambda i, j: (i, 0)),
                pl.BlockSpec((None, k, tn), lambda i, j: (layer, 0, j)), tile]
    out_specs, out_shape, args = tile, jax.ShapeDtypeStruct((m, n), jnp.float32), [a, w, x]
    if emit:
        gains, gl = next_gain
        in_specs.append(pl.BlockSpec((None, 1, tn), lambda i, j: (gl, 0, j)))
        args.append(gains)
        out_specs = [tile, tile, pl.BlockSpec((tm, V7X_LANES), lambda i, j: (i, 0))]
        out_shape = [out_shape, jax.ShapeDtypeStruct((m, n), ACT_DTYPE),
                     jax.ShapeDtypeStruct((m, V7X_LANES), jnp.float32)]
    return pl.pallas_call(
        functools.partial(_mm_resid_kernel, scale=scale, nj=nj, d=n, emit=emit),
        grid=(m // tm, nj),
        in_specs=in_specs, out_specs=out_specs, out_shape=out_shape,
        compiler_params=pltpu.CompilerParams(
            dimension_semantics=("arbitrary", "arbitrary"),
            vmem_limit_bytes=_vmem_limit(
                _nbytes((tm, k), a.dtype) + _nbytes((k, tn), w.dtype) + 3 * _nbytes((tm, tn), jnp.float32),
                2 * _nbytes((tm, tn), jnp.float32))),
        name=name,
    )(*args)


def _mm_nt_kernel(a_ref, r_ref, wt_ref, o_ref, *, act):
    y = _row_scale(r_ref, o_ref.shape[-1]) * _dot(a_ref[...], wt_ref[0], _NT)
    if act == "sigmoid":
        y = _sigmoid(y)
    o_ref[...] = y.astype(o_ref.dtype)


def _mm_nt(xg, r, wt, *, layer, row0, n, out_dtype, act, name):
    m, k = xg.shape
    tm, tn = _tile(m, 2048), _tile(n, 512)
    return pl.pallas_call(
        functools.partial(_mm_nt_kernel, act=act),
        grid=(m // tm, n // tn),
        in_specs=[pl.BlockSpec((tm, k), lambda i, j: (i, 0)),
                  pl.BlockSpec((tm, V7X_LANES), lambda i, j: (i, 0)),
                  pl.BlockSpec((pl.Element(1), pl.Element(tn), pl.Element(k)),
                               lambda i, j: (layer, pl.multiple_of(row0 + j * tn, 8), 0))],
        out_specs=pl.BlockSpec((tm, tn), lambda i, j: (i, j)),
        out_shape=jax.ShapeDtypeStruct((m, n), out_dtype),
        compiler_params=pltpu.CompilerParams(
            dimension_semantics=("arbitrary", "arbitrary"),
            vmem_limit_bytes=_vmem_limit(
                _nbytes((tm, k), xg.dtype) + _nbytes((tn, k), wt.dtype) + _nbytes((tm, tn), out_dtype),
                2 * _nbytes((tm, tn), jnp.float32))),
        name=name,
    )(xg, r, wt)


def _merge_kernel(r_ref, f_ref, l_ref, wr_ref, wf_ref, wl_ref, gr_ref, gf_ref, gl_ref, o_ref):
    y = gr_ref[...].astype(jnp.float32) * _dot(r_ref[...], wr_ref[...])
    y = y + gf_ref[...].astype(jnp.float32) * _dot(f_ref[...], wf_ref[...])
    y = y + gl_ref[...].astype(jnp.float32) * _dot(l_ref[...], wl_ref[...])
    o_ref[...] = y.astype(o_ref.dtype)


def _merge(ret, fox, lru, w_ret, w_fox, w_lru, gates, layer):
    m, kw = ret.shape
    d = w_ret.shape[-1]
    tm, tn = _tile(m, 1024), _tile(d, 512)
    nj = d // tn
    a_spec = pl.BlockSpec((tm, kw), lambda j, i: (i, 0))
    w_spec = pl.BlockSpec((None, kw, tn), lambda j, i: (layer, 0, j))
    g_specs = [pl.BlockSpec((tm, tn), functools.partial(lambda j, i, o: (i, o + j), o=b * nj))
               for b in range(N_BRANCH)]
    return pl.pallas_call(
        _merge_kernel,
        grid=(nj, m // tm),
        in_specs=[a_spec, a_spec, a_spec, w_spec, w_spec, w_spec] + g_specs,
        out_specs=pl.BlockSpec((tm, tn), lambda j, i: (i, j)),
        out_shape=jax.ShapeDtypeStruct((m, d), ACT_DTYPE),
        compiler_params=pltpu.CompilerParams(
            dimension_semantics=("arbitrary", "arbitrary"),
            vmem_limit_bytes=_vmem_limit(
                3 * _nbytes((tm, kw), ret.dtype) + 3 * _nbytes((kw, tn), w_ret.dtype) + 4 * _nbytes((tm, tn), ACT_DTYPE),
                4 * _nbytes((tm, tn), jnp.float32))),
        name="branch_merge",
    )(ret, fox, lru, w_ret, w_fox, w_lru, gates, gates, gates)


def _retention_kernel(q_ref, k_ref, v_ref, g_ref, cos_ref, sin_ref, gain_ref, o_ref, *, chunk, n_chunks):
    dh = q_ref.shape[-1]
    half = dh // 2
    f32 = jnp.float32
    h = pl.program_id(1).astype(f32)
    log_gamma = jnp.log(1.0 - jnp.exp2(jnp.full((1, 1), -5.0, f32) - h))
    row = lax.broadcasted_iota(jnp.int32, (chunk, chunk), 0)
    col = lax.broadcasted_iota(jnp.int32, (chunk, chunk), 1)
    rel = (row - col).astype(f32)
    inner_decay = jnp.where(rel >= 0, jnp.exp(jnp.maximum(rel, 0.0) * log_gamma), 0.0)
    idx = lax.broadcasted_iota(jnp.int32, (chunk, 1), 0).astype(f32)
    k_to_end = jnp.exp((chunk - 1.0 - idx) * log_gamma)
    q_from_start = jnp.exp((idx + 1.0) * log_gamma)
    chunk_decay = jnp.exp(chunk * log_gamma)
    gain = gain_ref[...]

    def rotary(x, cos, sin):
        x1, x2 = x[:, :half], x[:, half:]
        return jnp.concatenate([x1 * cos - x2 * sin, x2 * cos + x1 * sin], axis=-1)

    def body(n, state):
        rows = pl.ds(pl.multiple_of(n * chunk, chunk), chunk)
        cos, sin = cos_ref[rows, :], sin_ref[rows, :]
        q = rotary(q_ref[rows, :].astype(f32), cos, sin)
        k = rotary(k_ref[rows, :].astype(f32), cos, sin) * dh ** -0.5
        v = v_ref[rows, :]
        scores = _dot(q.astype(ACT_DTYPE), k.astype(ACT_DTYPE), _NT) * inner_decay
        inner = _dot(scores.astype(ACT_DTYPE), v)
        cross = _dot((q * q_from_start).astype(ACT_DTYPE), state.astype(ACT_DTYPE))
        kv = _dot((k * k_to_end).astype(ACT_DTYPE), v, _TN)
        o = inner + cross
        o = o * lax.rsqrt(jnp.mean(o * o, axis=-1, keepdims=True) + EPS) * gain
        g = g_ref[rows, :].astype(f32)
        o_ref[rows, :] = (g * _sigmoid(g) * o).astype(o_ref.dtype)
        return chunk_decay * state + kv

    lax.fori_loop(0, n_chunks, body, jnp.zeros((dh, dh), f32))


def _retention(p_ret, cos, sin, gains, layer, batch, seq, heads):
    m, w4 = p_ret.shape
    dh = w4 // (4 * heads)
    chunk = _tile(seq, 512)
    blk = lambda seg: pl.BlockSpec((seq, dh), lambda b, h: (b, seg * heads + h))
    tab = pl.BlockSpec((seq, dh // 2), lambda b, h: (0, 0))
    return pl.pallas_call(
        functools.partial(_retention_kernel, chunk=chunk, n_chunks=seq // chunk),
        grid=(batch, heads),
        in_specs=[blk(0), blk(1), blk(2), blk(3), tab, tab,
                  pl.BlockSpec((None, 1, dh), lambda b, h: (layer * heads + h, 0, 0))],
        out_specs=pl.BlockSpec((seq, dh), lambda b, h: (b, h)),
        out_shape=jax.ShapeDtypeStruct((m, heads * dh), ACT_DTYPE),
        compiler_params=pltpu.CompilerParams(
            dimension_semantics=("arbitrary", "arbitrary"),
            vmem_limit_bytes=_vmem_limit(
                5 * _nbytes((seq, dh), p_ret.dtype) + 2 * _nbytes((seq, dh // 2), jnp.float32),
                12 * _nbytes((chunk, max(chunk, dh)), jnp.float32))),
        name="retention",
    )(p_ret, p_ret, p_ret, p_ret, cos, sin, gains)


def _logf_cumsum_kernel(fg_ref, bias_ref, o_ref, *, chunk, n_chunks):
    f32 = jnp.float32
    row = lax.broadcasted_iota(jnp.int32, (chunk, chunk), 0)
    col = lax.broadcasted_iota(jnp.int32, (chunk, chunk), 1)
    upper = (row <= col).astype(ACT_DTYPE)
    bias = bias_ref[...]

    def body(n, carry):
        rows = pl.ds(pl.multiple_of(n * chunk, chunk), chunk)
        log_f = -_softplus(-(fg_ref[rows, :] + bias))
        p1 = log_f.astype(ACT_DTYPE)
        r1 = log_f - p1.astype(f32)
        p2 = r1.astype(ACT_DTYPE)
        p3 = (r1 - p2.astype(f32)).astype(ACT_DTYPE)
        cs = _dot(p1, upper, _TN) + _dot(p2, upper, _TN) + _dot(p3, upper, _TN)
        cs = cs + carry
        o_ref[:, rows] = cs
        return cs[:, chunk - 1:chunk]

    lax.fori_loop(0, n_chunks, body, jnp.zeros((fg_ref.shape[-1], 1), f32))


def _logf_cumsum(fg, bias, layer, batch, seq):
    lanes = fg.shape[-1]
    chunk = _tile(seq, 512)
    return pl.pallas_call(
        functools.partial(_logf_cumsum_kernel, chunk=chunk, n_chunks=seq // chunk),
        grid=(batch,),
        in_specs=[pl.BlockSpec((seq, lanes), lambda b: (b, 0)),
                  pl.BlockSpec((None, 1, lanes), lambda b: (layer, 0, 0))],
        out_specs=pl.BlockSpec((None, lanes, seq), lambda b: (b, 0, 0)),
        out_shape=jax.ShapeDtypeStruct((batch, lanes, seq), jnp.float32),
        compiler_params=pltpu.CompilerParams(
            dimension_semantics=("arbitrary",),
            vmem_limit_bytes=_vmem_limit(2 * _nbytes((seq, lanes), jnp.float32),
                                         8 * _nbytes((chunk, chunk), jnp.float32))),
        name="fox_logf_cumsum",
    )(fg, bias)


def _head_rms(x, gain):
    return x * lax.rsqrt(jnp.mean(x * x, axis=-1, keepdims=True) + EPS) * gain


FOX_HEADS_PER_STEP = 4
FOX_QUERY_SPLIT = 2


def _fox_kernel(q_ref, k_ref, v_ref, cum_ref, qg_ref, kg_ref, o_ref, kn_ref, va_ref, *, tq, nq, dh, scale):
    f32 = jnp.float32
    qi = pl.program_id(2)
    seq = k_ref.shape[0]
    nh = q_ref.shape[-1] // dh
    tr = tq // FOX_QUERY_SPLIT

    @pl.when(qi == 0)
    def _():
        ones_col = (lax.broadcasted_iota(jnp.int32, (seq, dh), 1) == 0).astype(va_ref.dtype)
        for hh in range(nh):
            cs = slice(hh * dh, (hh + 1) * dh)
            kn_ref[:, cs] = (_head_rms(k_ref[:, cs].astype(f32), kg_ref[...]) * (scale * LOG2_E)).astype(kn_ref.dtype)
            va_ref[:, 2 * hh * dh:(2 * hh + 1) * dh] = v_ref[:, cs]
            va_ref[:, (2 * hh + 1) * dh:(2 * hh + 2) * dh] = ones_col

    causal = (lax.broadcasted_iota(jnp.int32, (tr, tr), 0) >= lax.broadcasted_iota(jnp.int32, (tr, tr), 1))

    def one_chain(hh, r0, lo):
        cs = slice(hh * dh, (hh + 1) * dh)
        vs = slice(2 * hh * dh, (2 * hh + 2) * dh)
        q = _head_rms(q_ref[r0:r0 + tr, cs].astype(f32), qg_ref[...]).astype(ACT_DTYPE)

        def logits(a, b):
            return _dot(q, kn_ref[a:b, cs], _NT) - cum_ref[hh:hh + 1, a:b] * LOG2_E

        s_diag = jnp.where(causal, logits(lo, lo + tr), -jnp.inf)
        m = jnp.max(s_diag, axis=-1, keepdims=True)
        if lo:
            s_past = logits(0, lo)
            m = jnp.maximum(m, jnp.max(s_past, axis=-1, keepdims=True))
        acc = _dot(jnp.exp2(s_diag - m).astype(ACT_DTYPE), va_ref[lo:lo + tr, vs])
        if lo:
            acc = acc + _dot(jnp.exp2(s_past - m).astype(ACT_DTYPE), va_ref[0:lo, vs])
        o_ref[r0:r0 + tr, cs] = (acc[:, :dh] / acc[:, dh:dh + 1]).astype(o_ref.dtype)

    for case in range(nq):
        @pl.when(qi == case)
        def _(case=case):
            for hh in range(nh):
                for r0 in range(0, tq, tr):
                    one_chain(hh, r0, case * tq + r0)


def _fox(p_fox, cum, q_gain, k_gain, layer, batch, seq, heads):
    m, w3 = p_fox.shape
    dh = w3 // (3 * heads)
    hp = FOX_HEADS_PER_STEP if heads % FOX_HEADS_PER_STEP == 0 else 1
    hg = heads // hp
    tq = _tile(seq, 512)
    nq = seq // tq
    cum = cum.reshape(batch * hg, hp, seq)
    gain = pl.BlockSpec((None, 1, dh), lambda b, g, i: (layer, 0, 0))
    return pl.pallas_call(
        functools.partial(_fox_kernel, tq=tq, nq=nq, dh=dh, scale=dh ** -0.5),
        grid=(batch, hg, nq),
        in_specs=[pl.BlockSpec((tq, hp * dh), lambda b, g, i: (b * nq + i, g)),
                  pl.BlockSpec((seq, hp * dh), lambda b, g, i: (b, hg + g)),
                  pl.BlockSpec((seq, hp * dh), lambda b, g, i: (b, 2 * hg + g)),
                  pl.BlockSpec((None, hp, seq), lambda b, g, i: (b * hg + g, 0, 0)),
                  gain, gain],
        out_specs=pl.BlockSpec((tq, hp * dh), lambda b, g, i: (b * nq + i, g)),
        out_shape=jax.ShapeDtypeStruct((m, heads * dh), ACT_DTYPE),
        scratch_shapes=[pltpu.VMEM((seq, hp * dh), ACT_DTYPE), pltpu.VMEM((seq, 2 * hp * dh), ACT_DTYPE)],
        compiler_params=pltpu.CompilerParams(
            dimension_semantics=("arbitrary", "arbitrary", "arbitrary"),
            vmem_limit_bytes=_vmem_limit(3 * _nbytes((seq, hp * dh), p_fox.dtype),
                                         3 * _nbytes((seq, hp * dh), ACT_DTYPE) + 4 * hp * _nbytes((tq, seq), jnp.float32))),
        name="forgetting_attention",
    )(p_fox, p_fox, p_fox, cum, q_gain, k_gain)


LRU_PAD_ROWS = 8
LRU_CHUNK_ROWS = 32


def _lru_kernel(gate_ref, x_ref, cw_ref, prm_ref, wa_ref, wx_ref, o_ref, xs_ref, us_ref, as_ref, *, conv_width):
    f32 = jnp.float32
    seq, width = x_ref.shape
    pad = LRU_PAD_ROWS
    assert conv_width - 1 <= pad
    body = pl.ds(pad, seq)

    def shifted(ref, s):
        return ref[pl.ds(pad - s, seq), :]

    x = x_ref[...].astype(f32)
    xs_ref[0:pad, :] = jnp.zeros((pad, width), f32)
    xs_ref[body, :] = x
    conv_b, b_a, b_x, lam = (prm_ref[i:i + 1, :] for i in range(4))
    xc = conv_b + x * cw_ref[conv_width - 1:conv_width, :]
    for s in range(1, conv_width):
        xc = xc + shifted(xs_ref, s) * cw_ref[conv_width - 1 - s:conv_width - s, :]
    xcb = xc.astype(ACT_DTYPE)
    r = _sigmoid(_dot(xcb, wa_ref[...]) + b_a)
    i = _sigmoid(_dot(xcb, wx_ref[...]) + b_x)
    log_a = -LRU_C * r * _softplus(-lam)
    a = jnp.exp(log_a)
    w = -jnp.tanh(log_a) * (a * a + 1.0)
    u = jnp.where(w > 0.0, w * lax.rsqrt(w), 0.0) * (i * xc)
    chunk = min(LRU_CHUNK_ROWS, seq)
    h_chunks, carry = [], None
    for c0 in range(0, seq, chunk):
        uc, ac = u[c0:c0 + chunk], a[c0:c0 + chunk]
        base = (c0 // chunk) * (pad + chunk)
        rows = pl.ds(base + pad, chunk)
        us_ref[base:base + pad, :] = jnp.zeros((pad, width), f32)
        as_ref[base:base + pad, :] = jnp.ones((pad, width), f32)
        s = 1
        while s < min(pad, chunk):
            us_ref[rows, :] = uc
            as_ref[rows, :] = ac
            uc = ac * us_ref[pl.ds(base + pad - s, chunk), :] + uc
            ac = ac * as_ref[pl.ds(base + pad - s, chunk), :]
            s *= 2
        while s < chunk:
            uc = jnp.concatenate([uc[:s], ac[s:] * uc[:-s] + uc[s:]], axis=0)
            ac = jnp.concatenate([ac[:s], ac[s:] * ac[:-s]], axis=0)
            s *= 2
        if carry is not None:
            uc = ac * carry + uc
        carry = uc[chunk - 1:chunk]
        h_chunks.append(uc)
    h = jnp.concatenate(h_chunks, axis=0)
    o_ref[...] = (jax.nn.gelu(gate_ref[...].astype(f32)) * h).astype(o_ref.dtype)


def _lru(p_lru, conv_w, prm, w_a, w_x, layer, batch, seq):
    m, w2 = p_lru.shape
    nblk, bw = w_a.shape[1], w_a.shape[2]
    conv_width = conv_w.shape[1]
    chunk = min(LRU_CHUNK_ROWS, seq)
    wmat = pl.BlockSpec((None, None, bw, bw), lambda b, c: (layer, c, 0, 0))
    return pl.pallas_call(
        functools.partial(_lru_kernel, conv_width=conv_width),
        grid=(batch, nblk),
        in_specs=[pl.BlockSpec((seq, bw), lambda b, c: (b, c)),
                  pl.BlockSpec((seq, bw), lambda b, c: (b, nblk + c)),
                  pl.BlockSpec((None, conv_width, bw), lambda b, c: (layer, 0, c)),
                  pl.BlockSpec((None, 8, bw), lambda b, c: (layer, 0, c)),
                  wmat, wmat],
        out_specs=pl.BlockSpec((seq, bw), lambda b, c: (b, c)),
        out_shape=jax.ShapeDtypeStruct((m, w2 // 2), ACT_DTYPE),
        scratch_shapes=[pltpu.VMEM((LRU_PAD_ROWS + seq, bw), jnp.float32)]
        + [pltpu.VMEM((seq // chunk * (LRU_PAD_ROWS + chunk), bw), jnp.float32)] * 2,
        compiler_params=pltpu.CompilerParams(
            dimension_semantics=("arbitrary", "arbitrary"),
            vmem_limit_bytes=_vmem_limit(3 * _nbytes((seq, bw), p_lru.dtype),
                                         16 * _nbytes((seq, bw), jnp.float32))),
        name="rg_lru",
    )(p_lru, p_lru, conv_w, prm, w_a, w_x)


def kernel(x, ffn1_norm, ffn1_w_gate, ffn1_w_up, ffn1_w_down, mix_norm, w_in, ret_norm, fox_q_norm, fox_k_norm, fox_f_bias, lru_conv_w, lru_conv_b, lru_w_a, lru_b_a, lru_w_x, lru_b_x, lru_lambda, w_branch_ret, w_branch_fox, w_branch_lru, w_out, ffn2_norm, ffn2_w_gate, ffn2_w_up, ffn2_w_down):
    batch, seq, d = x.shape
    depth = w_in.shape[0]
    ret_heads, ret_dh = ret_norm.shape[1], ret_norm.shape[2]
    fox_heads, fox_dh = fox_f_bias.shape[1], fox_q_norm.shape[1]
    ret_w, fox_w, lru_w = ret_heads * ret_dh, fox_heads * fox_dh, lru_lambda.shape[1]
    assert fox_heads <= V7X_LANES
    c_fox = 4 * ret_w
    c_fg = c_fox + 3 * fox_w
    c_tail = c_fg + fox_heads
    assert w_in.shape[-1] == c_tail + 2 * lru_w + N_BRANCH * d

    f32 = jnp.float32
    as_rows = lambda p: p.reshape(depth, 1, -1)
    ffn1_g, mix_g, ffn2_g = as_rows(ffn1_norm), as_rows(mix_norm), as_rows(ffn2_norm)
    fox_qg, fox_kg = as_rows(fox_q_norm), as_rows(fox_k_norm)
    ret_g = ret_norm.reshape(depth * ret_heads, 1, ret_dh)
    fg_bias = as_rows(jnp.pad(fox_f_bias, ((0, 0), (0, V7X_LANES - fox_heads))))
    zeros = jnp.zeros_like(lru_lambda)
    lru_prm = jnp.stack([lru_conv_b, lru_b_a, lru_b_x, lru_lambda, zeros, zeros, zeros, zeros], axis=1)

    w_in_t = jnp.swapaxes(w_in, 1, 2)
    assert c_fg % 8 == 0 and c_tail % 8 == 0 and c_fg + V7X_LANES <= w_in.shape[-1]

    half = ret_dh // 2
    inv_freq = ROPE_BASE ** (-jnp.arange(half, dtype=f32) / half)
    ang = jnp.arange(seq).astype(f32)[:, None] * inv_freq[None, :]
    cos, sin = jnp.cos(ang), jnp.sin(ang)

    x = x.reshape(batch * seq, d)
    xg, r = _prep(x, ffn1_g, 0)
    for l in range(depth):
        h = _gateup(xg, r, ffn1_w_gate, ffn1_w_up, l)
        x, xg, r = _mm_resid(h, ffn1_w_down, x, l, HALF_STEP, "ffn_down", next_gain=(mix_g, l))

        proj = functools.partial(_mm_nt, xg, r, w_in_t, layer=l)
        p_ret = proj(row0=0, n=4 * ret_w, out_dtype=ACT_DTYPE, act=None, name="proj_ret")
        p_fox = proj(row0=c_fox, n=3 * fox_w, out_dtype=ACT_DTYPE, act=None, name="proj_fox")
        fg = proj(row0=c_fg, n=V7X_LANES, out_dtype=f32, act=None, name="proj_fgate")
        p_lru = proj(row0=c_tail, n=2 * lru_w, out_dtype=ACT_DTYPE, act=None, name="proj_lru")
        gates = proj(row0=c_tail + 2 * lru_w, n=N_BRANCH * d, out_dtype=ACT_DTYPE, act="sigmoid", name="proj_gates")

        ret = _retention(p_ret, cos, sin, ret_g, l, batch, seq, ret_heads)
        cum = _logf_cumsum(fg, fg_bias, l, batch, seq)[:, :fox_heads].reshape(batch * fox_heads, seq)
        fox = _fox(p_fox, cum, fox_qg, fox_kg, l, batch, seq, fox_heads)
        lru = _lru(p_lru, lru_conv_w, lru_prm, lru_w_a, lru_w_x, l, batch, seq)

        mixed = _merge(ret, fox, lru, w_branch_ret, w_branch_fox, w_branch_lru, gates, l)
        x, xg, r = _mm_resid(mixed, w_out, x, l, 1.0, "out_proj", next_gain=(ffn2_g, l))

        h = _gateup(xg, r, ffn2_w_gate, ffn2_w_up, l)
        if l + 1 < depth:
            x, xg, r = _mm_resid(h, ffn2_w_down, x, l, HALF_STEP, "ffn_down", next_gain=(ffn1_g, l + 1))
        else:
            x = _mm_resid(h, ffn2_w_down, x, l, HALF_STEP, "ffn_down")
    return x.reshape(batch, seq, d)
```

```python
import functools
import math

import jax
import jax.numpy as jnp
from jax import lax
from jax.experimental import pallas as pl
from jax.experimental.pallas import tpu as pltpu

EPS = 1e-6
HALF_STEP = 0.5
ROPE_BASE = 10000.0
LRU_C = 8.0
N_BRANCH = 3
LOG2_E = 1.4426950408889634

V7X_LANES = 128
V7X_VMEM_REQUEST_CAP = 62 * 1024 * 1024
ACT_DTYPE = jnp.bfloat16


def _vmem_limit(pipelined_bytes, resident_bytes=0):
    need = 2 * pipelined_bytes + resident_bytes
    return int(min(max(need, 16 * 1024 * 1024), V7X_VMEM_REQUEST_CAP))


def _nbytes(shape, dtype):
    return math.prod(shape) * jnp.dtype(dtype).itemsize


def _tile(n, pref):
    t = min(n, pref)
    assert n % t == 0, (n, pref)
    return t


def _dot(a, b, dims=(((1,), (0,)), ((), ()))):
    return lax.dot_general(a, b, dims, preferred_element_type=jnp.float32)


_NT = (((1,), (1,)), ((), ()))
_TN = (((0,), (0,)), ((), ()))


def _sigmoid(x):
    return 1.0 / (1.0 + jnp.exp(-x))


def _softplus(x):
    return jnp.maximum(x, 0.0) + jnp.log1p(jnp.exp(-jnp.abs(x)))


def _lane_partial(x):
    parts = [x[:, c:c + V7X_LANES] for c in range(0, x.shape[-1], V7X_LANES)]
    return functools.reduce(lambda p, q: p + q, parts)


def _row_scale(r_ref, width):
    r = r_ref[...]
    return r if width == V7X_LANES else jnp.concatenate([r] * (width // V7X_LANES), axis=-1)


def _emit_norm(x_new, gain_ref, xg_ref, r_ref, j, nj, d):
    xg_ref[...] = (x_new * gain_ref[...]).astype(xg_ref.dtype)
    r_ref[...] += _lane_partial(x_new * x_new)

    @pl.when(j == nj - 1)
    def _():
        ssq = jnp.sum(r_ref[...], axis=-1, keepdims=True)
        r_ref[...] = jnp.broadcast_to(lax.rsqrt(ssq * (1.0 / d) + EPS), r_ref.shape)


def _prep_kernel(x_ref, g_ref, xg_ref, r_ref):
    x = x_ref[...]
    xg_ref[...] = (x * g_ref[...]).astype(xg_ref.dtype)
    ms = jnp.mean(x * x, axis=-1, keepdims=True)
    r_ref[...] = jnp.broadcast_to(lax.rsqrt(ms + EPS), r_ref.shape)


def _prep(x, gains, layer):
    m, d = x.shape
    tr = _tile(m, 256)
    return pl.pallas_call(
        _prep_kernel,
        grid=(m // tr,),
        in_specs=[pl.BlockSpec((tr, d), lambda i: (i, 0)),
                  pl.BlockSpec((None, 1, d), lambda i: (layer, 0, 0))],
        out_specs=[pl.BlockSpec((tr, d), lambda i: (i, 0)),
                   pl.BlockSpec((tr, V7X_LANES), lambda i: (i, 0))],
        out_shape=[jax.ShapeDtypeStruct((m, d), ACT_DTYPE),
                   jax.ShapeDtypeStruct((m, V7X_LANES), jnp.float32)],
        compiler_params=pltpu.CompilerParams(
            dimension_semantics=("arbitrary",),
            vmem_limit_bytes=_vmem_limit(_nbytes((tr, d), jnp.float32) + _nbytes((tr, d), ACT_DTYPE),
                                         2 * _nbytes((tr, d), jnp.float32))),
        name="input_norm",
    )(x, gains)


def _gateup_kernel(a_ref, r_ref, wg_ref, wu_ref, o_ref):
    a = a_ref[...]
    r = _row_scale(r_ref, o_ref.shape[-1])
    g = r * _dot(a, wg_ref[...])
    u = r * _dot(a, wu_ref[...])
    o_ref[...] = (g * _sigmoid(g) * u).astype(o_ref.dtype)


def _gateup(xg, r, w_gate, w_up, layer):
    m, k = xg.shape
    f = w_gate.shape[-1]
    tm, tf = _tile(m, 2048), _tile(f, 256)
    w_spec = pl.BlockSpec((None, k, tf), lambda i, j: (layer, 0, j))
    return pl.pallas_call(
        _gateup_kernel,
        grid=(m // tm, f // tf),
        in_specs=[pl.BlockSpec((tm, k), lambda i, j: (i, 0)),
                  pl.BlockSpec((tm, V7X_LANES), lambda i, j: (i, 0)), w_spec, w_spec],
        out_specs=pl.BlockSpec((tm, tf), lambda i, j: (i, j)),
        out_shape=jax.ShapeDtypeStruct((m, f), ACT_DTYPE),
        compiler_params=pltpu.CompilerParams(
            dimension_semantics=("arbitrary", "arbitrary"),
            vmem_limit_bytes=_vmem_limit(
                _nbytes((tm, k), xg.dtype) + 2 * _nbytes((k, tf), w_gate.dtype) + _nbytes((tm, tf), ACT_DTYPE),
                4 * _nbytes((tm, tf), jnp.float32))),
        name="ffn_gateup",
    )(xg, r, w_gate, w_up)


def _mm_resid_kernel(a_ref, w_ref, x_ref, *rest, scale, nj, d, emit):
    if emit:
        gain_ref, o_ref, xg_ref, r_ref = rest
        j = pl.program_id(1)

        @pl.when(j == 0)
        def _():
            r_ref[...] = jnp.zeros(r_ref.shape, r_ref.dtype)
    else:
        (o_ref,) = rest
    x_new = x_ref[...] + scale * _dot(a_ref[...], w_ref[...])
    o_ref[...] = x_new
    if emit:
        _emit_norm(x_new, gain_ref, xg_ref, r_ref, j, nj, d)


def _mm_resid(a, w, x, layer, scale, name, next_gain=None):
    m, k = a.shape
    n = w.shape[-1]
    tm, tn = _tile(m, 1024), _tile(n, 512 if k <= 4096 else 256)
    nj = n // tn
    emit = next_gain is not None
    tile = pl.BlockSpec((tm, tn), lambda i, j: (i, j))
    in_specs = [pl.BlockSpec((tm, k), lambda i, j: (i, 0)),
                pl.BlockSpec((None, k, tn), lambda i, j: (layer, 0, j)), tile]
    out_specs, out_shape, args = tile, jax.ShapeDtypeStruct((m, n), jnp.float32), [a, w, x]
    if emit:
        gains, gl = next_gain
        in_specs.append(pl.BlockSpec((None, 1, tn), lambda i, j: (gl, 0, j)))
        args.append(gains)
        out_specs = [tile, tile, pl.BlockSpec((tm, V7X_LANES), lambda i, j: (i, 0))]
        out_shape = [out_shape, jax.ShapeDtypeStruct((m, n), ACT_DTYPE),
                     jax.ShapeDtypeStruct((m, V7X_LANES), jnp.float32)]
    return pl.pallas_call(
        functools.partial(_mm_resid_kernel, scale=scale, nj=nj, d=n, emit=emit),
        grid=(m // tm, nj),
        in_specs=in_specs, out_specs=out_specs, out_shape=out_shape,
        compiler_params=pltpu.CompilerParams(
            dimension_semantics=("arbitrary", "arbitrary"),
            vmem_limit_bytes=_vmem_limit(
                _nbytes((tm, k), a.dtype) + _nbytes((k, tn), w.dtype) + 3 * _nbytes((tm, tn), jnp.float32),
                2 * _nbytes((tm, tn), jnp.float32))),
        name=name,
    )(*args)


def _mm_nt_kernel(a_ref, r_ref, wt_ref, o_ref):
    y = _row_scale(r_ref, o_ref.shape[-1]) * _dot(a_ref[...], wt_ref[0], _NT)
    o_ref[...] = y.astype(o_ref.dtype)


def _mm_nt(xg, r, wt, *, layer, row0, n, out_dtype, name):
    m, k = xg.shape
    tm, tn = _tile(m, 2048), _tile(n, 512)
    return pl.pallas_call(
        _mm_nt_kernel,
        grid=(m // tm, n // tn),
        in_specs=[pl.BlockSpec((tm, k), lambda i, j: (i, 0)),
                  pl.BlockSpec((tm, V7X_LANES), lambda i, j: (i, 0)),
                  pl.BlockSpec((pl.Element(1), pl.Element(tn), pl.Element(k)),
                               lambda i, j: (layer, pl.multiple_of(row0 + j * tn, 8), 0))],
        out_specs=pl.BlockSpec((tm, tn), lambda i, j: (i, j)),
        out_shape=jax.ShapeDtypeStruct((m, n), out_dtype),
        compiler_params=pltpu.CompilerParams(
            dimension_semantics=("arbitrary", "arbitrary"),
            vmem_limit_bytes=_vmem_limit(
                _nbytes((tm, k), xg.dtype) + _nbytes((tn, k), wt.dtype) + _nbytes((tm, tn), out_dtype),
                2 * _nbytes((tm, tn), jnp.float32))),
        name=name,
    )(xg, r, wt)


def _merge_kernel(r_ref, f_ref, l_ref, wr_ref, wf_ref, wl_ref, gr_ref, gf_ref, gl_ref, o_ref):
    gate = lambda g_ref: _sigmoid(g_ref[...].astype(jnp.float32))
    y = gate(gr_ref) * _dot(r_ref[...], wr_ref[...])
    y = y + gate(gf_ref) * _dot(f_ref[...], wf_ref[...])
    y = y + gate(gl_ref) * _dot(l_ref[...], wl_ref[...])
    o_ref[...] = y.astype(o_ref.dtype)


def _merge(ret, fox, lru, w_ret, w_fox, w_lru, gates, gate_col0, layer):
    m, kw = ret.shape
    d = w_ret.shape[-1]
    tm, tn = _tile(m, 1024), _tile(d, 512)
    nj = d // tn
    assert gate_col0 % tn == 0
    a_spec = pl.BlockSpec((tm, kw), lambda j, i: (i, 0))
    w_spec = pl.BlockSpec((None, kw, tn), lambda j, i: (layer, 0, j))
    g_specs = [pl.BlockSpec((tm, tn), functools.partial(lambda j, i, o: (i, o + j), o=gate_col0 // tn + b * nj))
               for b in range(N_BRANCH)]
    return pl.pallas_call(
        _merge_kernel,
        grid=(nj, m // tm),
        in_specs=[a_spec, a_spec, a_spec, w_spec, w_spec, w_spec] + g_specs,
        out_specs=pl.BlockSpec((tm, tn), lambda j, i: (i, j)),
        out_shape=jax.ShapeDtypeStruct((m, d), ACT_DTYPE),
        compiler_params=pltpu.CompilerParams(
            dimension_semantics=("arbitrary", "arbitrary"),
            vmem_limit_bytes=_vmem_limit(
                3 * _nbytes((tm, kw), ret.dtype) + 3 * _nbytes((kw, tn), w_ret.dtype) + 4 * _nbytes((tm, tn), ACT_DTYPE),
                4 * _nbytes((tm, tn), jnp.float32))),
        name="branch_merge",
    )(ret, fox, lru, w_ret, w_fox, w_lru, gates, gates, gates)


def _retention_kernel(q_ref, k_ref, v_ref, g_ref, cos_ref, sin_ref, gain_ref, o_ref, *, chunk, n_chunks):
    dh = q_ref.shape[-1]
    half = dh // 2
    f32 = jnp.float32
    h = pl.program_id(1).astype(f32)
    log_gamma = jnp.log(1.0 - jnp.exp2(jnp.full((1, 1), -5.0, f32) - h))
    row = lax.broadcasted_iota(jnp.int32, (chunk, chunk), 0)
    col = lax.broadcasted_iota(jnp.int32, (chunk, chunk), 1)
    rel = (row - col).astype(f32)
    inner_decay = jnp.where(rel >= 0, jnp.exp(jnp.maximum(rel, 0.0) * log_gamma), 0.0)
    idx = lax.broadcasted_iota(jnp.int32, (chunk, 1), 0).astype(f32)
    k_to_end = jnp.exp((chunk - 1.0 - idx) * log_gamma)
    q_from_start = jnp.exp((idx + 1.0) * log_gamma)
    chunk_decay = jnp.exp(chunk * log_gamma)
    gain = gain_ref[...]

    def rotary(x, cos, sin):
        x1, x2 = x[:, :half], x[:, half:]
        return jnp.concatenate([x1 * cos - x2 * sin, x2 * cos + x1 * sin], axis=-1)

    def body(n, state):
        rows = pl.ds(pl.multiple_of(n * chunk, chunk), chunk)
        cos, sin = cos_ref[rows, :], sin_ref[rows, :]
        q = rotary(q_ref[rows, :].astype(f32), cos, sin)
        k = rotary(k_ref[rows, :].astype(f32), cos, sin) * dh ** -0.5
        v = v_ref[rows, :]
        scores = _dot(q.astype(ACT_DTYPE), k.astype(ACT_DTYPE), _NT) * inner_decay
        inner = _dot(scores.astype(ACT_DTYPE), v)
        cross = _dot((q * q_from_start).astype(ACT_DTYPE), state.astype(ACT_DTYPE))
        kv = _dot((k * k_to_end).astype(ACT_DTYPE), v, _TN)
        o = inner + cross
        o = o * lax.rsqrt(jnp.mean(o * o, axis=-1, keepdims=True) + EPS) * gain
        g = g_ref[rows, :].astype(f32)
        o_ref[rows, :] = (g * _sigmoid(g) * o).astype(o_ref.dtype)
        return chunk_decay * state + kv

    lax.fori_loop(0, n_chunks, body, jnp.zeros((dh, dh), f32))


def _retention(p_ret, cos, sin, gains, layer, batch, seq, heads):
    m = p_ret.shape[0]
    dh = gains.shape[-1]
    chunk = _tile(seq, 512)
    blk = lambda seg: pl.BlockSpec((seq, dh), lambda b, h: (b, seg * heads + h))
    tab = pl.BlockSpec((seq, dh // 2), lambda b, h: (0, 0))
    return pl.pallas_call(
        functools.partial(_retention_kernel, chunk=chunk, n_chunks=seq // chunk),
        grid=(batch, heads),
        in_specs=[blk(0), blk(1), blk(2), blk(3), tab, tab,
                  pl.BlockSpec((None, 1, dh), lambda b, h: (layer * heads + h, 0, 0))],
        out_specs=pl.BlockSpec((seq, dh), lambda b, h: (b, h)),
        out_shape=jax.ShapeDtypeStruct((m, heads * dh), ACT_DTYPE),
        compiler_params=pltpu.CompilerParams(
            dimension_semantics=("arbitrary", "arbitrary"),
            vmem_limit_bytes=_vmem_limit(
                5 * _nbytes((seq, dh), p_ret.dtype) + 2 * _nbytes((seq, dh // 2), jnp.float32),
                12 * _nbytes((chunk, max(chunk, dh)), jnp.float32))),
        name="retention",
    )(p_ret, p_ret, p_ret, p_ret, cos, sin, gains)


def _logf_cumsum_kernel(fg_ref, bias_ref, o_ref, *, chunk, n_chunks):
    f32 = jnp.float32
    row = lax.broadcasted_iota(jnp.int32, (chunk, chunk), 0)
    col = lax.broadcasted_iota(jnp.int32, (chunk, chunk), 1)
    upper = (row <= col).astype(ACT_DTYPE)
    bias = bias_ref[...]

    def body(n, carry):
        rows = pl.ds(pl.multiple_of(n * chunk, chunk), chunk)
        log_f = -_softplus(-(fg_ref[rows, :] + bias))
        p1 = log_f.astype(ACT_DTYPE)
        r1 = log_f - p1.astype(f32)
        p2 = r1.astype(ACT_DTYPE)
        p3 = (r1 - p2.astype(f32)).astype(ACT_DTYPE)
        cs = _dot(p1, upper, _TN) + _dot(p2, upper, _TN) + _dot(p3, upper, _TN)
        cs = cs + carry
        o_ref[:, rows] = cs
        return cs[:, chunk - 1:chunk]

    lax.fori_loop(0, n_chunks, body, jnp.zeros((fg_ref.shape[-1], 1), f32))


def _logf_cumsum(fg, bias, layer, batch, seq):
    lanes = fg.shape[-1]
    chunk = _tile(seq, 512)
    return pl.pallas_call(
        functools.partial(_logf_cumsum_kernel, chunk=chunk, n_chunks=seq // chunk),
        grid=(batch,),
        in_specs=[pl.BlockSpec((seq, lanes), lambda b: (b, 0)),
                  pl.BlockSpec((None, 1, lanes), lambda b: (layer, 0, 0))],
        out_specs=pl.BlockSpec((None, lanes, seq), lambda b: (b, 0, 0)),
        out_shape=jax.ShapeDtypeStruct((batch, lanes, seq), jnp.float32),
        compiler_params=pltpu.CompilerParams(
            dimension_semantics=("arbitrary",),
            vmem_limit_bytes=_vmem_limit(2 * _nbytes((seq, lanes), jnp.float32),
                                         8 * _nbytes((chunk, chunk), jnp.float32))),
        name="fox_logf_cumsum",
    )(fg, bias)


def _head_rms(x, gain):
    return x * lax.rsqrt(jnp.mean(x * x, axis=-1, keepdims=True) + EPS) * gain


FOX_HEADS_PER_STEP = 8
FOX_QUERY_SPLIT = 2


def _fox_kernel(q_ref, k_ref, v_ref, cum_ref, qg_ref, kg_ref, o_ref, kn_ref, va_ref, *, tq, nq, dh, scale):
    f32 = jnp.float32
    qi = pl.program_id(2)
    seq = k_ref.shape[0]
    nh = q_ref.shape[-1] // dh
    tr = tq // FOX_QUERY_SPLIT

    @pl.when(qi == 0)
    def _():
        ones_col = (lax.broadcasted_iota(jnp.int32, (seq, dh), 1) == 0).astype(va_ref.dtype)
        for hh in range(nh):
            cs = slice(hh * dh, (hh + 1) * dh)
            kn_ref[:, cs] = (_head_rms(k_ref[:, cs].astype(f32), kg_ref[...]) * (scale * LOG2_E)).astype(kn_ref.dtype)
            va_ref[:, 2 * hh * dh:(2 * hh + 1) * dh] = v_ref[:, cs]
            va_ref[:, (2 * hh + 1) * dh:(2 * hh + 2) * dh] = ones_col

    causal = (lax.broadcasted_iota(jnp.int32, (tr, tr), 0) >= lax.broadcasted_iota(jnp.int32, (tr, tr), 1))

    def one_chain(hh, r0, lo):
        cs = slice(hh * dh, (hh + 1) * dh)
        vs = slice(2 * hh * dh, (2 * hh + 2) * dh)
        q = _head_rms(q_ref[r0:r0 + tr, cs].astype(f32), qg_ref[...]).astype(ACT_DTYPE)

        def logits(a, b):
            return _dot(q, kn_ref[a:b, cs], _NT) - cum_ref[hh:hh + 1, a:b] * LOG2_E

        s_diag = jnp.where(causal, logits(lo, lo + tr), -jnp.inf)
        m = jnp.max(s_diag, axis=-1, keepdims=True)
        if lo:
            s_past = logits(0, lo)
            m = jnp.maximum(m, jnp.max(s_past, axis=-1, keepdims=True))
        acc = _dot(jnp.exp2(s_diag - m).astype(ACT_DTYPE), va_ref[lo:lo + tr, vs])
        if lo:
            acc = acc + _dot(jnp.exp2(s_past - m).astype(ACT_DTYPE), va_ref[0:lo, vs])
        o_ref[r0:r0 + tr, cs] = (acc[:, :dh] / acc[:, dh:dh + 1]).astype(o_ref.dtype)

    for case in range(nq):
        @pl.when(qi == case)
        def _(case=case):
            for hh in range(nh):
                for r0 in range(0, tq, tr):
                    one_chain(hh, r0, case * tq + r0)


def _fox(p, col0, cum, q_gain, k_gain, layer, batch, seq, heads):
    m = p.shape[0]
    dh = q_gain.shape[-1]
    hp = FOX_HEADS_PER_STEP if heads % FOX_HEADS_PER_STEP == 0 else 1
    hg = heads // hp
    assert col0 % (hp * dh) == 0
    g0 = col0 // (hp * dh)
    tq = _tile(seq, 512)
    nq = seq // tq
    cum = cum.reshape(batch * hg, hp, seq)
    gain = pl.BlockSpec((None, 1, dh), lambda b, g, i: (layer, 0, 0))
    return pl.pallas_call(
        functools.partial(_fox_kernel, tq=tq, nq=nq, dh=dh, scale=dh ** -0.5),
        grid=(batch, hg, nq),
        in_specs=[pl.BlockSpec((tq, hp * dh), lambda b, g, i: (b * nq + i, g0 + g)),
                  pl.BlockSpec((seq, hp * dh), lambda b, g, i: (b, g0 + hg + g)),
                  pl.BlockSpec((seq, hp * dh), lambda b, g, i: (b, g0 + 2 * hg + g)),
                  pl.BlockSpec((None, hp, seq), lambda b, g, i: (b * hg + g, 0, 0)),
                  gain, gain],
        out_specs=pl.BlockSpec((tq, hp * dh), lambda b, g, i: (b * nq + i, g)),
        out_shape=jax.ShapeDtypeStruct((m, heads * dh), ACT_DTYPE),
        scratch_shapes=[pltpu.VMEM((seq, hp * dh), ACT_DTYPE), pltpu.VMEM((seq, 2 * hp * dh), ACT_DTYPE)],
        compiler_params=pltpu.CompilerParams(
            dimension_semantics=("arbitrary", "arbitrary", "arbitrary"),
            vmem_limit_bytes=_vmem_limit(3 * _nbytes((seq, hp * dh), p.dtype),
                                         3 * _nbytes((seq, hp * dh), ACT_DTYPE) + 4 * hp * _nbytes((tq, seq), jnp.float32))),
        name="forgetting_attention",
    )(p, p, p, cum, q_gain, k_gain)


LRU_PAD_ROWS = 8
LRU_CHUNK_ROWS = 32


def _lru_kernel(gate_ref, x_ref, cw_ref, prm_ref, wa_ref, wx_ref, o_ref, xs_ref, us_ref, as_ref, *, conv_width):
    f32 = jnp.float32
    seq, width = x_ref.shape
    pad = LRU_PAD_ROWS
    assert conv_width - 1 <= pad
    body = pl.ds(pad, seq)

    def shifted(ref, s):
        return ref[pl.ds(pad - s, seq), :]

    x = x_ref[...].astype(f32)
    xs_ref[0:pad, :] = jnp.zeros((pad, width), f32)
    xs_ref[body, :] = x
    conv_b, b_a, b_x, lam = (prm_ref[i:i + 1, :] for i in range(4))
    xc = conv_b + x * cw_ref[conv_width - 1:conv_width, :]
    for s in range(1, conv_width):
        xc = xc + shifted(xs_ref, s) * cw_ref[conv_width - 1 - s:conv_width - s, :]
    xcb = xc.astype(ACT_DTYPE)
    r = _sigmoid(_dot(xcb, wa_ref[...]) + b_a)
    i = _sigmoid(_dot(xcb, wx_ref[...]) + b_x)
    log_a = -LRU_C * r * _softplus(-lam)
    a = jnp.exp(log_a)
    w = -jnp.tanh(log_a) * (a * a + 1.0)
    u = jnp.where(w > 0.0, w * lax.rsqrt(w), 0.0) * (i * xc)
    chunk = min(LRU_CHUNK_ROWS, seq)
    h_chunks, carry = [], None
    for c0 in range(0, seq, chunk):
        uc, ac = u[c0:c0 + chunk], a[c0:c0 + chunk]
        base = (c0 // chunk) * (pad + chunk)
        rows = pl.ds(base + pad, chunk)
        us_ref[base:base + pad, :] = jnp.zeros((pad, width), f32)
        as_ref[base:base + pad, :] = jnp.ones((pad, width), f32)
        s = 1
        while s < min(pad, chunk):
            us_ref[rows, :] = uc
            as_ref[rows, :] = ac
            uc = ac * us_ref[pl.ds(base + pad - s, chunk), :] + uc
            ac = ac * as_ref[pl.ds(base + pad - s, chunk), :]
            s *= 2
        while s < chunk:
            uc = jnp.concatenate([uc[:s], ac[s:] * uc[:-s] + uc[s:]], axis=0)
            ac = jnp.concatenate([ac[:s], ac[s:] * ac[:-s]], axis=0)
            s *= 2
        if carry is not None:
            uc = ac * carry + uc
        carry = uc[chunk - 1:chunk]
        h_chunks.append(uc)
    h = jnp.concatenate(h_chunks, axis=0)
    o_ref[...] = (jax.nn.gelu(gate_ref[...].astype(f32)) * h).astype(o_ref.dtype)


def _lru(p_lru, conv_w, prm, w_a, w_x, layer, batch, seq):
    m = p_lru.shape[0]
    nblk, bw = w_a.shape[1], w_a.shape[2]
    conv_width = conv_w.shape[1]
    chunk = min(LRU_CHUNK_ROWS, seq)
    wmat = pl.BlockSpec((None, None, bw, bw), lambda b, c: (layer, c, 0, 0))
    return pl.pallas_call(
        functools.partial(_lru_kernel, conv_width=conv_width),
        grid=(batch, nblk),
        in_specs=[pl.BlockSpec((seq, bw), lambda b, c: (b, c)),
                  pl.BlockSpec((seq, bw), lambda b, c: (b, nblk + c)),
                  pl.BlockSpec((None, conv_width, bw), lambda b, c: (layer, 0, c)),
                  pl.BlockSpec((None, 8, bw), lambda b, c: (layer, 0, c)),
                  wmat, wmat],
        out_specs=pl.BlockSpec((seq, bw), lambda b, c: (b, c)),
        out_shape=jax.ShapeDtypeStruct((m, nblk * bw), ACT_DTYPE),
        scratch_shapes=[pltpu.VMEM((LRU_PAD_ROWS + seq, bw), jnp.float32)]
        + [pltpu.VMEM((seq // chunk * (LRU_PAD_ROWS + chunk), bw), jnp.float32)] * 2,
        compiler_params=pltpu.CompilerParams(
            dimension_semantics=("arbitrary", "arbitrary"),
            vmem_limit_bytes=_vmem_limit(3 * _nbytes((seq, bw), p_lru.dtype),
                                         16 * _nbytes((seq, bw), jnp.float32))),
        name="rg_lru",
    )(p_lru, p_lru, conv_w, prm, w_a, w_x)


def kernel(x, ffn1_norm, ffn1_w_gate, ffn1_w_up, ffn1_w_down, mix_norm, w_in, ret_norm, fox_q_norm, fox_k_norm, fox_f_bias, lru_conv_w, lru_conv_b, lru_w_a, lru_b_a, lru_w_x, lru_b_x, lru_lambda, w_branch_ret, w_branch_fox, w_branch_lru, w_out, ffn2_norm, ffn2_w_gate, ffn2_w_up, ffn2_w_down):
    batch, seq, d = x.shape
    depth = w_in.shape[0]
    ret_heads, ret_dh = ret_norm.shape[1], ret_norm.shape[2]
    fox_heads, fox_dh = fox_f_bias.shape[1], fox_q_norm.shape[1]
    ret_w, fox_w, lru_w = ret_heads * ret_dh, fox_heads * fox_dh, lru_lambda.shape[1]
    assert fox_heads <= V7X_LANES
    c_fox = 4 * ret_w
    c_fg = c_fox + 3 * fox_w
    c_tail = c_fg + fox_heads
    assert w_in.shape[-1] == c_tail + 2 * lru_w + N_BRANCH * d

    f32 = jnp.float32
    as_rows = lambda p: p.reshape(depth, 1, -1)
    ffn1_g, mix_g, ffn2_g = as_rows(ffn1_norm), as_rows(mix_norm), as_rows(ffn2_norm)
    fox_qg, fox_kg = as_rows(fox_q_norm), as_rows(fox_k_norm)
    ret_g = ret_norm.reshape(depth * ret_heads, 1, ret_dh)
    fg_bias = as_rows(jnp.pad(fox_f_bias, ((0, 0), (0, V7X_LANES - fox_heads))))
    zeros = jnp.zeros_like(lru_lambda)
    lru_prm = jnp.stack([lru_conv_b, lru_b_a, lru_b_x, lru_lambda, zeros, zeros, zeros, zeros], axis=1)

    w_in_t = jnp.swapaxes(w_in, 1, 2)
    assert c_fg % 8 == 0 and c_tail % 8 == 0 and c_fg + V7X_LANES <= w_in.shape[-1]

    half = ret_dh // 2
    inv_freq = ROPE_BASE ** (-jnp.arange(half, dtype=f32) / half)
    ang = jnp.arange(seq).astype(f32)[:, None] * inv_freq[None, :]
    cos, sin = jnp.cos(ang), jnp.sin(ang)

    x = x.reshape(batch * seq, d)
    xg, r = _prep(x, ffn1_g, 0)
    for l in range(depth):
        h = _gateup(xg, r, ffn1_w_gate, ffn1_w_up, l)
        x, xg, r = _mm_resid(h, ffn1_w_down, x, l, HALF_STEP, "ffn_down", next_gain=(mix_g, l))

        proj = functools.partial(_mm_nt, xg, r, w_in_t, layer=l)
        p_head = proj(row0=0, n=c_fg, out_dtype=ACT_DTYPE, name="proj_ret_fox")
        p_tail = proj(row0=c_tail, n=2 * lru_w + N_BRANCH * d, out_dtype=ACT_DTYPE, name="proj_lru_gates")
        fg = proj(row0=c_fg, n=V7X_LANES, out_dtype=f32, name="proj_fgate")

        ret = _retention(p_head, cos, sin, ret_g, l, batch, seq, ret_heads)
        cum = _logf_cumsum(fg, fg_bias, l, batch, seq)[:, :fox_heads].reshape(batch * fox_heads, seq)
        fox = _fox(p_head, c_fox, cum, fox_qg, fox_kg, l, batch, seq, fox_heads)
        lru = _lru(p_tail, lru_conv_w, lru_prm, lru_w_a, lru_w_x, l, batch, seq)

        mixed = _merge(ret, fox, lru, w_branch_ret, w_branch_fox, w_branch_lru, p_tail, 2 * lru_w, l)
        x, xg, r = _mm_resid(mixed, w_out, x, l, 1.0, "out_proj", next_gain=(ffn2_g, l))

        h = _gateup(xg, r, ffn2_w_gate, ffn2_w_up, l)
        if l + 1 < depth:
            x, xg, r = _mm_resid(h, ffn2_w_down, x, l, HALF_STEP, "ffn_down", next_gain=(ffn1_g, l + 1))
        else:
            x = _mm_resid(h, ffn2_w_down, x, l, HALF_STEP, "ffn_down")
    return x.reshape(batch, seq, d)
```

```python
import functools
import math

import jax
import jax.numpy as jnp
from jax import lax
from jax.experimental import pallas as pl
from jax.experimental.pallas import tpu as pltpu

EPS = 1e-6
HALF_STEP = 0.5
ROPE_BASE = 10000.0
LRU_C = 8.0
N_BRANCH = 3
LOG2_E = 1.4426950408889634

V7X_LANES = 128
V7X_VMEM_REQUEST_CAP = 62 * 1024 * 1024
ACT_DTYPE = jnp.bfloat16


def _vmem_limit(pipelined_bytes, resident_bytes=0):
    need = 2 * pipelined_bytes + resident_bytes
    return int(min(max(need, 16 * 1024 * 1024), V7X_VMEM_REQUEST_CAP))


def _nbytes(shape, dtype):
    return math.prod(shape) * jnp.dtype(dtype).itemsize


def _tile(n, pref):
    t = min(n, pref)
    assert n % t == 0, (n, pref)
    return t


def _dot(a, b, dims=(((1,), (0,)), ((), ()))):
    return lax.dot_general(a, b, dims, preferred_element_type=jnp.float32)


_NT = (((1,), (1,)), ((), ()))
_TN = (((0,), (0,)), ((), ()))


def _sigmoid(x):
    return 1.0 / (1.0 + jnp.exp(-x))


def _softplus(x):
    return jnp.maximum(x, 0.0) + jnp.log1p(jnp.exp(-jnp.abs(x)))


def _lane_partial(x):
    parts = [x[:, c:c + V7X_LANES] for c in range(0, x.shape[-1], V7X_LANES)]
    return functools.reduce(lambda p, q: p + q, parts)


def _row_scale(r_ref, width):
    r = r_ref[...]
    return r if width == V7X_LANES else jnp.concatenate([r] * (width // V7X_LANES), axis=-1)


def _emit_norm(x_new, gain_ref, xg_ref, r_ref, j, nj, d):
    xg_ref[...] = (x_new * gain_ref[...]).astype(xg_ref.dtype)
    r_ref[...] += _lane_partial(x_new * x_new)

    @pl.when(j == nj - 1)
    def _():
        ssq = jnp.sum(r_ref[...], axis=-1, keepdims=True)
        r_ref[...] = jnp.broadcast_to(lax.rsqrt(ssq * (1.0 / d) + EPS), r_ref.shape)


def _prep_kernel(x_ref, g_ref, xg_ref, r_ref):
    x = x_ref[...]
    xg_ref[...] = (x * g_ref[...]).astype(xg_ref.dtype)
    ms = jnp.mean(x * x, axis=-1, keepdims=True)
    r_ref[...] = jnp.broadcast_to(lax.rsqrt(ms + EPS), r_ref.shape)


def _prep(x, gains, layer):
    m, d = x.shape
    tr = _tile(m, 256)
    return pl.pallas_call(
        _prep_kernel,
        grid=(m // tr,),
        in_specs=[pl.BlockSpec((tr, d), lambda i: (i, 0)),
                  pl.BlockSpec((None, 1, d), lambda i: (layer, 0, 0))],
        out_specs=[pl.BlockSpec((tr, d), lambda i: (i, 0)),
                   pl.BlockSpec((tr, V7X_LANES), lambda i: (i, 0))],
        out_shape=[jax.ShapeDtypeStruct((m, d), ACT_DTYPE),
                   jax.ShapeDtypeStruct((m, V7X_LANES), jnp.float32)],
        compiler_params=pltpu.CompilerParams(
            dimension_semantics=("arbitrary",),
            vmem_limit_bytes=_vmem_limit(_nbytes((tr, d), jnp.float32) + _nbytes((tr, d), ACT_DTYPE),
                                         2 * _nbytes((tr, d), jnp.float32))),
        name="input_norm",
    )(x, gains)


def _gateup_kernel(a_ref, r_ref, wg_ref, wu_ref, o_ref):
    a = a_ref[...]
    r = _row_scale(r_ref, o_ref.shape[-1])
    g = r * _dot(a, wg_ref[...])
    u = r * _dot(a, wu_ref[...])
    o_ref[...] = (g * _sigmoid(g) * u).astype(o_ref.dtype)


def _gateup(xg, r, w_gate, w_up, layer):
    m, k = xg.shape
    f = w_gate.shape[-1]
    tm, tf = _tile(m, 2048), _tile(f, 256)
    w_spec = pl.BlockSpec((None, k, tf), lambda i, j: (layer, 0, j))
    return pl.pallas_call(
        _gateup_kernel,
        grid=(m // tm, f // tf),
        in_specs=[pl.BlockSpec((tm, k), lambda i, j: (i, 0)),
                  pl.BlockSpec((tm, V7X_LANES), lambda i, j: (i, 0)), w_spec, w_spec],
        out_specs=pl.BlockSpec((tm, tf), lambda i, j: (i, j)),
        out_shape=jax.ShapeDtypeStruct((m, f), ACT_DTYPE),
        compiler_params=pltpu.CompilerParams(
            dimension_semantics=("arbitrary", "arbitrary"),
            vmem_limit_bytes=_vmem_limit(
                _nbytes((tm, k), xg.dtype) + 2 * _nbytes((k, tf), w_gate.dtype) + _nbytes((tm, tf), ACT_DTYPE),
                4 * _nbytes((tm, tf), jnp.float32))),
        name="ffn_gateup",
    )(xg, r, w_gate, w_up)


def _mm_resid_kernel(a_ref, w_ref, x_ref, *rest, scale, nj, d, emit):
    if emit:
        gain_ref, o_ref, xg_ref, r_ref = rest
        j = pl.program_id(1)

        @pl.when(j == 0)
        def _():
            r_ref[...] = jnp.zeros(r_ref.shape, r_ref.dtype)
    else:
        (o_ref,) = rest
    x_new = x_ref[...] + scale * _dot(a_ref[...], w_ref[...])
    o_ref[...] = x_new
    if emit:
        _emit_norm(x_new, gain_ref, xg_ref, r_ref, j, nj, d)


def _mm_resid(a, w, x, layer, scale, name, next_gain=None):
    m, k = a.shape
    n = w.shape[-1]
    tm, tn = _tile(m, 1024), _tile(n, 512 if k <= 4096 else 256)
    nj = n // tn
    emit = next_gain is not None
    tile = pl.BlockSpec((tm, tn), lambda i, j: (i, j))
    in_specs = [pl.BlockSpec((tm, k), lambda i, j: (i, 0)),
                pl.BlockSpec((None, k, tn), lambda i, j: (layer, 0, j)), tile]
    out_specs, out_shape, args = tile, jax.ShapeDtypeStruct((m, n), jnp.float32), [a, w, x]
    if emit:
        gains, gl = next_gain
        in_specs.append(pl.BlockSpec((None, 1, tn), lambda i, j: (gl, 0, j)))
        args.append(gains)
        out_specs = [tile, tile, pl.BlockSpec((tm, V7X_LANES), lambda i, j: (i, 0))]
        out_shape = [out_shape, jax.ShapeDtypeStruct((m, n), ACT_DTYPE),
                     jax.ShapeDtypeStruct((m, V7X_LANES), jnp.float32)]
    return pl.pallas_call(
        functools.partial(_mm_resid_kernel, scale=scale, nj=nj, d=n, emit=emit),
        grid=(m // tm, nj),
        in_specs=in_specs, out_specs=out_specs, out_shape=out_shape,
        compiler_params=pltpu.CompilerParams(
            dimension_semantics=("arbitrary", "arbitrary"),
            vmem_limit_bytes=_vmem_limit(
                _nbytes((tm, k), a.dtype) + _nbytes((k, tn), w.dtype) + 3 * _nbytes((tm, tn), jnp.float32),
                2 * _nbytes((tm, tn), jnp.float32))),
        name=name,
    )(*args)


def _mm_nt_kernel(a_ref, r_ref, wt_ref, o_ref):
    y = _row_scale(r_ref, o_ref.shape[-1]) * _dot(a_ref[...], wt_ref[0], _NT)
    o_ref[...] = y.astype(o_ref.dtype)


def _mm_nt(xg, r, wt, *, layer, row0, n, out_dtype, name):
    m, k = xg.shape
    tm, tn = _tile(m, 2048), _tile(n, 512)
    return pl.pallas_call(
        _mm_nt_kernel,
        grid=(m // tm, n // tn),
        in_specs=[pl.BlockSpec((tm, k), lambda i, j: (i, 0)),
                  pl.BlockSpec((tm, V7X_LANES), lambda i, j: (i, 0)),
                  pl.BlockSpec((pl.Element(1), pl.Element(tn), pl.Element(k)),
                               lambda i, j: (layer, pl.multiple_of(row0 + j * tn, 8), 0))],
        out_specs=pl.BlockSpec((tm, tn), lambda i, j: (i, j)),
        out_shape=jax.ShapeDtypeStruct((m, n), out_dtype),
        compiler_params=pltpu.CompilerParams(
            dimension_semantics=("arbitrary", "arbitrary"),
            vmem_limit_bytes=_vmem_limit(
                _nbytes((tm, k), xg.dtype) + _nbytes((tn, k), wt.dtype) + _nbytes((tm, tn), out_dtype),
                2 * _nbytes((tm, tn), jnp.float32))),
        name=name,
    )(xg, r, wt)


def _merge_kernel(r_ref, f_ref, l_ref, wr_ref, wf_ref, wl_ref, gr_ref, gf_ref, gl_ref, o_ref):
    gate = lambda g_ref: _sigmoid(g_ref[...].astype(jnp.float32))
    y = gate(gr_ref) * _dot(r_ref[...], wr_ref[...])
    y = y + gate(gf_ref) * _dot(f_ref[...], wf_ref[...])
    y = y + gate(gl_ref) * _dot(l_ref[...], wl_ref[...])
    o_ref[...] = y.astype(o_ref.dtype)


def _merge(ret, fox, lru, w_ret, w_fox, w_lru, gates, gate_col0, layer):
    m, kw = ret.shape
    d = w_ret.shape[-1]
    tm, tn = _tile(m, 1024), _tile(d, 512)
    nj = d // tn
    assert gate_col0 % tn == 0
    a_spec = pl.BlockSpec((tm, kw), lambda j, i: (i, 0))
    w_spec = pl.BlockSpec((None, kw, tn), lambda j, i: (layer, 0, j))
    g_specs = [pl.BlockSpec((tm, tn), functools.partial(lambda j, i, o: (i, o + j), o=gate_col0 // tn + b * nj))
               for b in range(N_BRANCH)]
    return pl.pallas_call(
        _merge_kernel,
        grid=(nj, m // tm),
        in_specs=[a_spec, a_spec, a_spec, w_spec, w_spec, w_spec] + g_specs,
        out_specs=pl.BlockSpec((tm, tn), lambda j, i: (i, j)),
        out_shape=jax.ShapeDtypeStruct((m, d), ACT_DTYPE),
        compiler_params=pltpu.CompilerParams(
            dimension_semantics=("arbitrary", "arbitrary"),
            vmem_limit_bytes=_vmem_limit(
                3 * _nbytes((tm, kw), ret.dtype) + 3 * _nbytes((kw, tn), w_ret.dtype) + 4 * _nbytes((tm, tn), ACT_DTYPE),
                4 * _nbytes((tm, tn), jnp.float32))),
        name="branch_merge",
    )(ret, fox, lru, w_ret, w_fox, w_lru, gates, gates, gates)


RET_HEADS_PER_STEP = 2


def _retention_kernel(q_ref, k_ref, v_ref, g_ref, cos_ref, sin_ref, gain_ref, o_ref, *, chunk, n_chunks, dh):
    half = dh // 2
    f32 = jnp.float32
    nh = q_ref.shape[-1] // dh
    row = lax.broadcasted_iota(jnp.int32, (chunk, chunk), 0)
    col = lax.broadcasted_iota(jnp.int32, (chunk, chunk), 1)
    rel = (row - col).astype(f32)
    idx = lax.broadcasted_iota(jnp.int32, (chunk, 1), 0).astype(f32)

    def decays(hh):
        h = (pl.program_id(1) * nh + hh).astype(f32)
        log_gamma = jnp.log(1.0 - jnp.exp2(jnp.full((1, 1), -5.0, f32) - h))
        inner_decay = jnp.where(rel >= 0, jnp.exp(jnp.maximum(rel, 0.0) * log_gamma), 0.0)
        k_to_end = jnp.exp((chunk - 1.0 - idx) * log_gamma)
        q_from_start = jnp.exp((idx + 1.0) * log_gamma)
        chunk_decay = jnp.exp(chunk * log_gamma)
        return inner_decay, k_to_end, q_from_start, chunk_decay

    tables = [decays(hh) for hh in range(nh)]

    def rotary(x, cos, sin):
        x1, x2 = x[:, :half], x[:, half:]
        return jnp.concatenate([x1 * cos - x2 * sin, x2 * cos + x1 * sin], axis=-1)

    def one_head(hh, rows, cos, sin, state):
        inner_decay, k_to_end, q_from_start, chunk_decay = tables[hh]
        cs = slice(hh * dh, (hh + 1) * dh)
        q = rotary(q_ref[rows, cs].astype(f32), cos, sin)
        k = rotary(k_ref[rows, cs].astype(f32), cos, sin) * dh ** -0.5
        v = v_ref[rows, cs]
        scores = _dot(q.astype(ACT_DTYPE), k.astype(ACT_DTYPE), _NT) * inner_decay
        inner = _dot(scores.astype(ACT_DTYPE), v)
        cross = _dot((q * q_from_start).astype(ACT_DTYPE), state.astype(ACT_DTYPE))
        kv = _dot((k * k_to_end).astype(ACT_DTYPE), v, _TN)
        o = inner + cross
        o = o * lax.rsqrt(jnp.mean(o * o, axis=-1, keepdims=True) + EPS) * gain_ref[hh]
        g = g_ref[rows, cs].astype(f32)
        o_ref[rows, cs] = (g * _sigmoid(g) * o).astype(o_ref.dtype)
        return chunk_decay * state + kv

    def body(n, states):
        rows = pl.ds(pl.multiple_of(n * chunk, chunk), chunk)
        cos, sin = cos_ref[rows, :], sin_ref[rows, :]
        return tuple(one_head(hh, rows, cos, sin, states[hh]) for hh in range(nh))

    lax.fori_loop(0, n_chunks, body, tuple(jnp.zeros((dh, dh), f32) for _ in range(nh)))


def _retention(p_ret, cos, sin, gains, layer, batch, seq, heads):
    m = p_ret.shape[0]
    dh = gains.shape[-1]
    hp = RET_HEADS_PER_STEP if heads % RET_HEADS_PER_STEP == 0 else 1
    hg = heads // hp
    chunk = _tile(seq, 512)
    blk = lambda seg: pl.BlockSpec((seq, hp * dh), lambda b, g: (b, seg * hg + g))
    tab = pl.BlockSpec((seq, dh // 2), lambda b, g: (0, 0))
    return pl.pallas_call(
        functools.partial(_retention_kernel, chunk=chunk, n_chunks=seq // chunk, dh=dh),
        grid=(batch, hg),
        in_specs=[blk(0), blk(1), blk(2), blk(3), tab, tab,
                  pl.BlockSpec((hp, 1, dh), lambda b, g: (layer * hg + g, 0, 0))],
        out_specs=pl.BlockSpec((seq, hp * dh), lambda b, g: (b, g)),
        out_shape=jax.ShapeDtypeStruct((m, heads * dh), ACT_DTYPE),
        compiler_params=pltpu.CompilerParams(
            dimension_semantics=("arbitrary", "arbitrary"),
            vmem_limit_bytes=_vmem_limit(
                5 * _nbytes((seq, hp * dh), p_ret.dtype) + 2 * _nbytes((seq, dh // 2), jnp.float32),
                12 * hp * _nbytes((chunk, max(chunk, dh)), jnp.float32))),
        name="retention",
    )(p_ret, p_ret, p_ret, p_ret, cos, sin, gains)


def _logf_cumsum_kernel(fg_ref, bias_ref, o_ref, *, chunk, n_chunks):
    f32 = jnp.float32
    row = lax.broadcasted_iota(jnp.int32, (chunk, chunk), 0)
    col = lax.broadcasted_iota(jnp.int32, (chunk, chunk), 1)
    upper = (row <= col).astype(ACT_DTYPE)
    bias = bias_ref[...]

    def body(n, carry):
        rows = pl.ds(pl.multiple_of(n * chunk, chunk), chunk)
        log_f = -_softplus(-(fg_ref[rows, :] + bias))
        p1 = log_f.astype(ACT_DTYPE)
        r1 = log_f - p1.astype(f32)
        p2 = r1.astype(ACT_DTYPE)
        p3 = (r1 - p2.astype(f32)).astype(ACT_DTYPE)
        cs = _dot(p1, upper, _TN) + _dot(p2, upper, _TN) + _dot(p3, upper, _TN)
        cs = cs + carry
        o_ref[:, rows] = cs
        return cs[:, chunk - 1:chunk]

    lax.fori_loop(0, n_chunks, body, jnp.zeros((fg_ref.shape[-1], 1), f32))


def _logf_cumsum(fg, bias, layer, batch, seq):
    lanes = fg.shape[-1]
    chunk = _tile(seq, 512)
    return pl.pallas_call(
        functools.partial(_logf_cumsum_kernel, chunk=chunk, n_chunks=seq // chunk),
        grid=(batch,),
        in_specs=[pl.BlockSpec((seq, lanes), lambda b: (b, 0)),
                  pl.BlockSpec((None, 1, lanes), lambda b: (layer, 0, 0))],
        out_specs=pl.BlockSpec((None, lanes, seq), lambda b: (b, 0, 0)),
        out_shape=jax.ShapeDtypeStruct((batch, lanes, seq), jnp.float32),
        compiler_params=pltpu.CompilerParams(
            dimension_semantics=("arbitrary",),
            vmem_limit_bytes=_vmem_limit(2 * _nbytes((seq, lanes), jnp.float32),
                                         8 * _nbytes((chunk, chunk), jnp.float32))),
        name="fox_logf_cumsum",
    )(fg, bias)


def _head_rms(x, gain):
    return x * lax.rsqrt(jnp.mean(x * x, axis=-1, keepdims=True) + EPS) * gain


FOX_HEADS_PER_STEP = 4
FOX_QUERY_SPLIT = 2


def _fox_kernel(q_ref, k_ref, v_ref, cum_ref, qg_ref, kg_ref, o_ref, kn_ref, va_ref, *, tq, nq, dh, scale):
    f32 = jnp.float32
    qi = pl.program_id(2)
    seq = k_ref.shape[0]
    nh = q_ref.shape[-1] // dh
    tr = tq // FOX_QUERY_SPLIT

    @pl.when(qi == 0)
    def _():
        ones_col = (lax.broadcasted_iota(jnp.int32, (seq, dh), 1) == 0).astype(va_ref.dtype)
        for hh in range(nh):
            cs = slice(hh * dh, (hh + 1) * dh)
            kn_ref[:, cs] = (_head_rms(k_ref[:, cs].astype(f32), kg_ref[...]) * (scale * LOG2_E)).astype(kn_ref.dtype)
            va_ref[:, 2 * hh * dh:(2 * hh + 1) * dh] = v_ref[:, cs]
            va_ref[:, (2 * hh + 1) * dh:(2 * hh + 2) * dh] = ones_col

    causal = (lax.broadcasted_iota(jnp.int32, (tr, tr), 0) >= lax.broadcasted_iota(jnp.int32, (tr, tr), 1))

    def one_chain(hh, r0, lo):
        cs = slice(hh * dh, (hh + 1) * dh)
        vs = slice(2 * hh * dh, (2 * hh + 2) * dh)
        q = _head_rms(q_ref[r0:r0 + tr, cs].astype(f32), qg_ref[...]).astype(ACT_DTYPE)

        def logits(a, b):
            return _dot(q, kn_ref[a:b, cs], _NT) - cum_ref[hh:hh + 1, a:b] * LOG2_E

        s_diag = jnp.where(causal, logits(lo, lo + tr), -jnp.inf)
        m = jnp.max(s_diag, axis=-1, keepdims=True)
        if lo:
            s_past = logits(0, lo)
            m = jnp.maximum(m, jnp.max(s_past, axis=-1, keepdims=True))
        acc = _dot(jnp.exp2(s_diag - m).astype(ACT_DTYPE), va_ref[lo:lo + tr, vs])
        if lo:
            acc = acc + _dot(jnp.exp2(s_past - m).astype(ACT_DTYPE), va_ref[0:lo, vs])
        o_ref[r0:r0 + tr, cs] = (acc[:, :dh] / acc[:, dh:dh + 1]).astype(o_ref.dtype)

    for case in range(nq):
        @pl.when(qi == case)
        def _(case=case):
            for hh in range(nh):
                for r0 in range(0, tq, tr):
                    one_chain(hh, r0, case * tq + r0)


def _fox(p, col0, cum, q_gain, k_gain, layer, batch, seq, heads):
    m = p.shape[0]
    dh = q_gain.shape[-1]
    hp = FOX_HEADS_PER_STEP if heads % FOX_HEADS_PER_STEP == 0 else 1
    hg = heads // hp
    assert col0 % (hp * dh) == 0
    g0 = col0 // (hp * dh)
    tq = _tile(seq, 512)
    nq = seq // tq
    cum = cum.reshape(batch * hg, hp, seq)
    gain = pl.BlockSpec((None, 1, dh), lambda b, g, i: (layer, 0, 0))
    return pl.pallas_call(
        functools.partial(_fox_kernel, tq=tq, nq=nq, dh=dh, scale=dh ** -0.5),
        grid=(batch, hg, nq),
        in_specs=[pl.BlockSpec((tq, hp * dh), lambda b, g, i: (b * nq + i, g0 + g)),
                  pl.BlockSpec((seq, hp * dh), lambda b, g, i: (b, g0 + hg + g)),
                  pl.BlockSpec((seq, hp * dh), lambda b, g, i: (b, g0 + 2 * hg + g)),
                  pl.BlockSpec((None, hp, seq), lambda b, g, i: (b * hg + g, 0, 0)),
                  gain, gain],
        out_specs=pl.BlockSpec((tq, hp * dh), lambda b, g, i: (b * nq + i, g)),
        out_shape=jax.ShapeDtypeStruct((m, heads * dh), ACT_DTYPE),
        scratch_shapes=[pltpu.VMEM((seq, hp * dh), ACT_DTYPE), pltpu.VMEM((seq, 2 * hp * dh), ACT_DTYPE)],
        compiler_params=pltpu.CompilerParams(
            dimension_semantics=("arbitrary", "arbitrary", "arbitrary"),
            vmem_limit_bytes=_vmem_limit(3 * _nbytes((seq, hp * dh), p.dtype),
                                         3 * _nbytes((seq, hp * dh), ACT_DTYPE) + 4 * hp * _nbytes((tq, seq), jnp.float32))),
        name="forgetting_attention",
    )(p, p, p, cum, q_gain, k_gain)


LRU_PAD_ROWS = 8
LRU_CHUNK_ROWS = 32


def _lru_kernel(gate_ref, x_ref, cw_ref, prm_ref, wa_ref, wx_ref, o_ref, xs_ref, us_ref, as_ref, *, conv_width):
    f32 = jnp.float32
    seq, width = x_ref.shape
    pad = LRU_PAD_ROWS
    assert conv_width - 1 <= pad
    body = pl.ds(pad, seq)

    def shifted(ref, s):
        return ref[pl.ds(pad - s, seq), :]

    x = x_ref[...].astype(f32)
    xs_ref[0:pad, :] = jnp.zeros((pad, width), f32)
    xs_ref[body, :] = x
    conv_b, b_a, b_x, lam = (prm_ref[i:i + 1, :] for i in range(4))
    xc = conv_b + x * cw_ref[conv_width - 1:conv_width, :]
    for s in range(1, conv_width):
        xc = xc + shifted(xs_ref, s) * cw_ref[conv_width - 1 - s:conv_width - s, :]
    xcb = xc.astype(ACT_DTYPE)
    r = _sigmoid(_dot(xcb, wa_ref[...]) + b_a)
    i = _sigmoid(_dot(xcb, wx_ref[...]) + b_x)
    log_a = -LRU_C * r * _softplus(-lam)
    a = jnp.exp(log_a)
    w = -jnp.tanh(log_a) * (a * a + 1.0)
    u = jnp.where(w > 0.0, w * lax.rsqrt(w), 0.0) * (i * xc)
    chunk = min(LRU_CHUNK_ROWS, seq)
    h_chunks, carry = [], None
    for c0 in range(0, seq, chunk):
        uc, ac = u[c0:c0 + chunk], a[c0:c0 + chunk]
        base = (c0 // chunk) * (pad + chunk)
        rows = pl.ds(base + pad, chunk)
        us_ref[base:base + pad, :] = jnp.zeros((pad, width), f32)
        as_ref[base:base + pad, :] = jnp.ones((pad, width), f32)
        s = 1
        while s < min(pad, chunk):
            us_ref[rows, :] = uc
            as_ref[rows, :] = ac
            uc = ac * us_ref[pl.ds(base + pad - s, chunk), :] + uc
            ac = ac * as_ref[pl.ds(base + pad - s, chunk), :]
            s *= 2
        while s < chunk:
            uc = jnp.concatenate([uc[:s], ac[s:] * uc[:-s] + uc[s:]], axis=0)
            ac = jnp.concatenate([ac[:s], ac[s:] * ac[:-s]], axis=0)
            s *= 2
        if carry is not None:
            uc = ac * carry + uc
        carry = uc[chunk - 1:chunk]
        h_chunks.append(uc)
    h = jnp.concatenate(h_chunks, axis=0)
    o_ref[...] = (jax.nn.gelu(gate_ref[...].astype(f32)) * h).astype(o_ref.dtype)


def _lru(p_lru, conv_w, prm, w_a, w_x, layer, batch, seq):
    m = p_lru.shape[0]
    nblk, bw = w_a.shape[1], w_a.shape[2]
    conv_width = conv_w.shape[1]
    chunk = min(LRU_CHUNK_ROWS, seq)
    wmat = pl.BlockSpec((None, None, bw, bw), lambda b, c: (layer, c, 0, 0))
    return pl.pallas_call(
        functools.partial(_lru_kernel, conv_width=conv_width),
        grid=(batch, nblk),
        in_specs=[pl.BlockSpec((seq, bw), lambda b, c: (b, c)),
                  pl.BlockSpec((seq, bw), lambda b, c: (b, nblk + c)),
                  pl.BlockSpec((None, conv_width, bw), lambda b, c: (layer, 0, c)),
                  pl.BlockSpec((None, 8, bw), lambda b, c: (layer, 0, c)),
                  wmat, wmat],
        out_specs=pl.BlockSpec((seq, bw), lambda b, c: (b, c)),
        out_shape=jax.ShapeDtypeStruct((m, nblk * bw), ACT_DTYPE),
        scratch_shapes=[pltpu.VMEM((LRU_PAD_ROWS + seq, bw), jnp.float32)]
        + [pltpu.VMEM((seq // chunk * (LRU_PAD_ROWS + chunk), bw), jnp.float32)] * 2,
        compiler_params=pltpu.CompilerParams(
            dimension_semantics=("arbitrary", "arbitrary"),
            vmem_limit_bytes=_vmem_limit(3 * _nbytes((seq, bw), p_lru.dtype),
                                         16 * _nbytes((seq, bw), jnp.float32))),
        name="rg_lru",
    )(p_lru, p_lru, conv_w, prm, w_a, w_x)


def kernel(x, ffn1_norm, ffn1_w_gate, ffn1_w_up, ffn1_w_down, mix_norm, w_in, ret_norm, fox_q_norm, fox_k_norm, fox_f_bias, lru_conv_w, lru_conv_b, lru_w_a, lru_b_a, lru_w_x, lru_b_x, lru_lambda, w_branch_ret, w_branch_fox, w_branch_lru, w_out, ffn2_norm, ffn2_w_gate, ffn2_w_up, ffn2_w_down):
    batch, seq, d = x.shape
    depth = w_in.shape[0]
    ret_heads, ret_dh = ret_norm.shape[1], ret_norm.shape[2]
    fox_heads, fox_dh = fox_f_bias.shape[1], fox_q_norm.shape[1]
    ret_w, fox_w, lru_w = ret_heads * ret_dh, fox_heads * fox_dh, lru_lambda.shape[1]
    assert fox_heads <= V7X_LANES
    c_fox = 4 * ret_w
    c_fg = c_fox + 3 * fox_w
    c_tail = c_fg + fox_heads
    assert w_in.shape[-1] == c_tail + 2 * lru_w + N_BRANCH * d

    f32 = jnp.float32
    as_rows = lambda p: p.reshape(depth, 1, -1)
    ffn1_g, mix_g, ffn2_g = as_rows(ffn1_norm), as_rows(mix_norm), as_rows(ffn2_norm)
    fox_qg, fox_kg = as_rows(fox_q_norm), as_rows(fox_k_norm)
    ret_g = ret_norm.reshape(depth * ret_heads, 1, ret_dh)
    fg_bias = as_rows(jnp.pad(fox_f_bias, ((0, 0), (0, V7X_LANES - fox_heads))))
    zeros = jnp.zeros_like(lru_lambda)
    lru_prm = jnp.stack([lru_conv_b, lru_b_a, lru_b_x, lru_lambda, zeros, zeros, zeros, zeros], axis=1)

    w_in_t = jnp.swapaxes(w_in, 1, 2)
    assert c_fg % 8 == 0 and c_tail % 8 == 0 and c_fg + V7X_LANES <= w_in.shape[-1]

    half = ret_dh // 2
    inv_freq = ROPE_BASE ** (-jnp.arange(half, dtype=f32) / half)
    ang = jnp.arange(seq).astype(f32)[:, None] * inv_freq[None, :]
    cos, sin = jnp.cos(ang), jnp.sin(ang)

    x = x.reshape(batch * seq, d)
    xg, r = _prep(x, ffn1_g, 0)
    for l in range(depth):
        h = _gateup(xg, r, ffn1_w_gate, ffn1_w_up, l)
        x, xg, r = _mm_resid(h, ffn1_w_down, x, l, HALF_STEP, "ffn_down", next_gain=(mix_g, l))

        proj = functools.partial(_mm_nt, xg, r, w_in_t, layer=l)
        p_head = proj(row0=0, n=c_fg, out_dtype=ACT_DTYPE, name="proj_ret_fox")
        p_tail = proj(row0=c_tail, n=2 * lru_w + N_BRANCH * d, out_dtype=ACT_DTYPE, name="proj_lru_gates")
        fg = proj(row0=c_fg, n=V7X_LANES, out_dtype=f32, name="proj_fgate")

        ret = _retention(p_head, cos, sin, ret_g, l, batch, seq, ret_heads)
        cum = _logf_cumsum(fg, fg_bias, l, batch, seq)[:, :fox_heads].reshape(batch * fox_heads, seq)
        fox = _fox(p_head, c_fox, cum, fox_qg, fox_kg, l, batch, seq, fox_heads)
        lru = _lru(p_tail, lru_conv_w, lru_prm, lru_w_a, lru_w_x, l, batch, seq)

        mixed = _merge(ret, fox, lru, w_branch_ret, w_branch_fox, w_branch_lru, p_tail, 2 * lru_w, l)
        x, xg, r = _mm_resid(mixed, w_out, x, l, 1.0, "out_proj", next_gain=(ffn2_g, l))

        h = _gateup(xg, r, ffn2_w_gate, ffn2_w_up, l)
        if l + 1 < depth:
            x, xg, r = _mm_resid(h, ffn2_w_down, x, l, HALF_STEP, "ffn_down", next_gain=(ffn1_g, l + 1))
        else:
            x = _mm_resid(h, ffn2_w_down, x, l, HALF_STEP, "ffn_down")
    return x.reshape(batch, seq, d)
```

```python
import functools
import math

import jax
import jax.numpy as jnp
from jax import lax
from jax.experimental import pallas as pl
from jax.experimental.pallas import tpu as pltpu

EPS = 1e-6
HALF_STEP = 0.5
ROPE_BASE = 10000.0
LRU_C = 8.0
N_BRANCH = 3
LOG2_E = 1.4426950408889634

V7X_LANES = 128
V7X_VMEM_REQUEST_CAP = 62 * 1024 * 1024
ACT_DTYPE = jnp.bfloat16


def _vmem_limit(pipelined_bytes, resident_bytes=0):
    need = 2 * pipelined_bytes + resident_bytes
    return int(min(max(need, 16 * 1024 * 1024), V7X_VMEM_REQUEST_CAP))


def _nbytes(shape, dtype):
    return math.prod(shape) * jnp.dtype(dtype).itemsize


def _tile(n, pref):
    t = min(n, pref)
    assert n % t == 0, (n, pref)
    return t


def _dot(a, b, dims=(((1,), (0,)), ((), ()))):
    return lax.dot_general(a, b, dims, preferred_element_type=jnp.float32)


_NT = (((1,), (1,)), ((), ()))
_TN = (((0,), (0,)), ((), ()))


def _sigmoid(x):
    return 1.0 / (1.0 + jnp.exp(-x))


def _softplus(x):
    return jnp.maximum(x, 0.0) + jnp.log1p(jnp.exp(-jnp.abs(x)))


def _lane_partial(x):
    parts = [x[:, c:c + V7X_LANES] for c in range(0, x.shape[-1], V7X_LANES)]
    return functools.reduce(lambda p, q: p + q, parts)


def _row_scale(r_ref, width):
    r = r_ref[...]
    return r if width == V7X_LANES else jnp.concatenate([r] * (width // V7X_LANES), axis=-1)


def _emit_norm(x_new, gain_ref, xg_ref, r_ref, j, nj, d):
    xg_ref[...] = (x_new * gain_ref[...]).astype(xg_ref.dtype)
    r_ref[...] += _lane_partial(x_new * x_new)

    @pl.when(j == nj - 1)
    def _():
        ssq = jnp.sum(r_ref[...], axis=-1, keepdims=True)
        r_ref[...] = jnp.broadcast_to(lax.rsqrt(ssq * (1.0 / d) + EPS), r_ref.shape)


def _prep_kernel(x_ref, g_ref, xg_ref, r_ref):
    x = x_ref[...]
    xg_ref[...] = (x * g_ref[...]).astype(xg_ref.dtype)
    ms = jnp.mean(x * x, axis=-1, keepdims=True)
    r_ref[...] = jnp.broadcast_to(lax.rsqrt(ms + EPS), r_ref.shape)


def _prep(x, gains, layer):
    m, d = x.shape
    tr = _tile(m, 256)
    return pl.pallas_call(
        _prep_kernel,
        grid=(m // tr,),
        in_specs=[pl.BlockSpec((tr, d), lambda i: (i, 0)),
                  pl.BlockSpec((None, 1, d), lambda i: (layer, 0, 0))],
        out_specs=[pl.BlockSpec((tr, d), lambda i: (i, 0)),
                   pl.BlockSpec((tr, V7X_LANES), lambda i: (i, 0))],
        out_shape=[jax.ShapeDtypeStruct((m, d), ACT_DTYPE),
                   jax.ShapeDtypeStruct((m, V7X_LANES), jnp.float32)],
        compiler_params=pltpu.CompilerParams(
            dimension_semantics=("arbitrary",),
            vmem_limit_bytes=_vmem_limit(_nbytes((tr, d), jnp.float32) + _nbytes((tr, d), ACT_DTYPE),
                                         2 * _nbytes((tr, d), jnp.float32))),
        name="input_norm",
    )(x, gains)


def _gateup_kernel(a_ref, r_ref, wg_ref, wu_ref, o_ref):
    a = a_ref[...]
    r = _row_scale(r_ref, o_ref.shape[-1])
    g = r * _dot(a, wg_ref[...])
    u = r * _dot(a, wu_ref[...])
    o_ref[...] = (g * _sigmoid(g) * u).astype(o_ref.dtype)


def _gateup(xg, r, w_gate, w_up, layer):
    m, k = xg.shape
    f = w_gate.shape[-1]
    tm, tf = _tile(m, 2048), _tile(f, 256)
    w_spec = pl.BlockSpec((None, k, tf), lambda i, j: (layer, 0, j))
    return pl.pallas_call(
        _gateup_kernel,
        grid=(m // tm, f // tf),
        in_specs=[pl.BlockSpec((tm, k), lambda i, j: (i, 0)),
                  pl.BlockSpec((tm, V7X_LANES), lambda i, j: (i, 0)), w_spec, w_spec],
        out_specs=pl.BlockSpec((tm, tf), lambda i, j: (i, j)),
        out_shape=jax.ShapeDtypeStruct((m, f), ACT_DTYPE),
        compiler_params=pltpu.CompilerParams(
            dimension_semantics=("arbitrary", "arbitrary"),
            vmem_limit_bytes=_vmem_limit(
                _nbytes((tm, k), xg.dtype) + 2 * _nbytes((k, tf), w_gate.dtype) + _nbytes((tm, tf), ACT_DTYPE),
                4 * _nbytes((tm, tf), jnp.float32))),
        name="ffn_gateup",
    )(xg, r, w_gate, w_up)


def _mm_resid_kernel(a_ref, w_ref, x_ref, *rest, scale, nj, d, emit):
    if emit:
        gain_ref, o_ref, xg_ref, r_ref = rest
        j = pl.program_id(1)

        @pl.when(j == 0)
        def _():
            r_ref[...] = jnp.zeros(r_ref.shape, r_ref.dtype)
    else:
        (o_ref,) = rest
    x_new = x_ref[...] + scale * _dot(a_ref[...], w_ref[...])
    o_ref[...] = x_new
    if emit:
        _emit_norm(x_new, gain_ref, xg_ref, r_ref, j, nj, d)


def _mm_resid(a, w, x, layer, scale, name, next_gain=None):
    m, k = a.shape
    n = w.shape[-1]
    tm, tn = _tile(m, 1024), _tile(n, 512 if k <= 4096 else 256)
    nj = n // tn
    emit = next_gain is not None
    tile = pl.BlockSpec((tm, tn), lambda i, j: (i, j))
    in_specs = [pl.BlockSpec((tm, k), lambda i, j: (i, 0)),
                pl.BlockSpec((None, k, tn), lambda i, j: (layer, 0, j)), tile]
    out_specs, out_shape, args = tile, jax.ShapeDtypeStruct((m, n), jnp.float32), [a, w, x]
    if emit:
        gains, gl = next_gain
        in_specs.append(pl.BlockSpec((None, 1, tn), lambda i, j: (gl, 0, j)))
        args.append(gains)
        out_specs = [tile, tile, pl.BlockSpec((tm, V7X_LANES), lambda i, j: (i, 0))]
        out_shape = [out_shape, jax.ShapeDtypeStruct((m, n), ACT_DTYPE),
                     jax.ShapeDtypeStruct((m, V7X_LANES), jnp.float32)]
    return pl.pallas_call(
        functools.partial(_mm_resid_kernel, scale=scale, nj=nj, d=n, emit=emit),
        grid=(m // tm, nj),
        in_specs=in_specs, out_specs=out_specs, out_shape=out_shape,
        compiler_params=pltpu.CompilerParams(
            dimension_semantics=("arbitrary", "arbitrary"),
            vmem_limit_bytes=_vmem_limit(
                _nbytes((tm, k), a.dtype) + _nbytes((k, tn), w.dtype) + 3 * _nbytes((tm, tn), jnp.float32),
                2 * _nbytes((tm, tn), jnp.float32))),
        name=name,
    )(*args)


def _mm_nt_kernel(a_ref, r_ref, wt_ref, o_ref):
    y = _row_scale(r_ref, o_ref.shape[-1]) * _dot(a_ref[...], wt_ref[0], _NT)
    o_ref[...] = y.astype(o_ref.dtype)


def _mm_nt(xg, r, wt, *, layer, row0, n, out_dtype, name):
    m, k = xg.shape
    tm, tn = _tile(m, 2048), _tile(n, 512)
    return pl.pallas_call(
        _mm_nt_kernel,
        grid=(m // tm, n // tn),
        in_specs=[pl.BlockSpec((tm, k), lambda i, j: (i, 0)),
                  pl.BlockSpec((tm, V7X_LANES), lambda i, j: (i, 0)),
                  pl.BlockSpec((pl.Element(1), pl.Element(tn), pl.Element(k)),
                               lambda i, j: (layer, pl.multiple_of(row0 + j * tn, 8), 0))],
        out_specs=pl.BlockSpec((tm, tn), lambda i, j: (i, j)),
        out_shape=jax.ShapeDtypeStruct((m, n), out_dtype),
        compiler_params=pltpu.CompilerParams(
            dimension_semantics=("arbitrary", "arbitrary"),
            vmem_limit_bytes=_vmem_limit(
                _nbytes((tm, k), xg.dtype) + _nbytes((tn, k), wt.dtype) + _nbytes((tm, tn), out_dtype),
                2 * _nbytes((tm, tn), jnp.float32))),
        name=name,
    )(xg, r, wt)


def _merge_kernel(r_ref, f_ref, l_ref, wr_ref, wf_ref, wl_ref, gr_ref, gf_ref, gl_ref, o_ref):
    gate = lambda g_ref: _sigmoid(g_ref[...].astype(jnp.float32))
    y = gate(gr_ref) * _dot(r_ref[...], wr_ref[...])
    y = y + gate(gf_ref) * _dot(f_ref[...], wf_ref[...])
    y = y + gate(gl_ref) * _dot(l_ref[...], wl_ref[...])
    o_ref[...] = y.astype(o_ref.dtype)


def _merge(ret, fox, lru, w_ret, w_fox, w_lru, gates, gate_col0, layer):
    m, kw = ret.shape
    d = w_ret.shape[-1]
    tm, tn = _tile(m, 1024), _tile(d, 512)
    nj = d // tn
    assert gate_col0 % tn == 0
    a_spec = pl.BlockSpec((tm, kw), lambda j, i: (i, 0))
    w_spec = pl.BlockSpec((None, kw, tn), lambda j, i: (layer, 0, j))
    g_specs = [pl.BlockSpec((tm, tn), functools.partial(lambda j, i, o: (i, o + j), o=gate_col0 // tn + b * nj))
               for b in range(N_BRANCH)]
    return pl.pallas_call(
        _merge_kernel,
        grid=(nj, m // tm),
        in_specs=[a_spec, a_spec, a_spec, w_spec, w_spec, w_spec] + g_specs,
        out_specs=pl.BlockSpec((tm, tn), lambda j, i: (i, j)),
        out_shape=jax.ShapeDtypeStruct((m, d), ACT_DTYPE),
        compiler_params=pltpu.CompilerParams(
            dimension_semantics=("arbitrary", "arbitrary"),
            vmem_limit_bytes=_vmem_limit(
                3 * _nbytes((tm, kw), ret.dtype) + 3 * _nbytes((kw, tn), w_ret.dtype) + 4 * _nbytes((tm, tn), ACT_DTYPE),
                4 * _nbytes((tm, tn), jnp.float32))),
        name="branch_merge",
    )(ret, fox, lru, w_ret, w_fox, w_lru, gates, gates, gates)


RET_HEADS_PER_STEP = 2


def _retention_kernel(q_ref, k_ref, v_ref, g_ref, cos_ref, sin_ref, gain_ref, o_ref, *, chunk, n_chunks, dh):
    half = dh // 2
    f32 = jnp.float32
    nh = q_ref.shape[-1] // dh
    row = lax.broadcasted_iota(jnp.int32, (chunk, chunk), 0)
    col = lax.broadcasted_iota(jnp.int32, (chunk, chunk), 1)
    rel = (row - col).astype(f32)
    idx = lax.broadcasted_iota(jnp.int32, (chunk, 1), 0).astype(f32)

    def decays(hh):
        h = (pl.program_id(1) * nh + hh).astype(f32)
        log_gamma = jnp.log(1.0 - jnp.exp2(jnp.full((1, 1), -5.0, f32) - h))
        inner_decay = jnp.where(rel >= 0, jnp.exp(jnp.maximum(rel, 0.0) * log_gamma), 0.0)
        k_to_end = jnp.exp((chunk - 1.0 - idx) * log_gamma)
        q_from_start = jnp.exp((idx + 1.0) * log_gamma)
        chunk_decay = jnp.exp(chunk * log_gamma)
        return inner_decay, k_to_end, q_from_start, chunk_decay

    tables = [decays(hh) for hh in range(nh)]

    def rotary(x, cos, sin):
        x1, x2 = x[:, :half], x[:, half:]
        return jnp.concatenate([x1 * cos - x2 * sin, x2 * cos + x1 * sin], axis=-1)

    def one_head(hh, rows, cos, sin, state):
        inner_decay, k_to_end, q_from_start, chunk_decay = tables[hh]
        cs = slice(hh * dh, (hh + 1) * dh)
        q = rotary(q_ref[rows, cs].astype(f32), cos, sin)
        k = rotary(k_ref[rows, cs].astype(f32), cos, sin) * dh ** -0.5
        v = v_ref[rows, cs]
        scores = _dot(q.astype(ACT_DTYPE), k.astype(ACT_DTYPE), _NT) * inner_decay
        inner = _dot(scores.astype(ACT_DTYPE), v)
        cross = _dot((q * q_from_start).astype(ACT_DTYPE), state.astype(ACT_DTYPE))
        kv = _dot((k * k_to_end).astype(ACT_DTYPE), v, _TN)
        o = inner + cross
        o = o * lax.rsqrt(jnp.mean(o * o, axis=-1, keepdims=True) + EPS) * gain_ref[hh]
        g = g_ref[rows, cs].astype(f32)
        o_ref[rows, cs] = (g * _sigmoid(g) * o).astype(o_ref.dtype)
        return chunk_decay * state + kv

    def body(n, states):
        rows = pl.ds(pl.multiple_of(n * chunk, chunk), chunk)
        cos, sin = cos_ref[rows, :], sin_ref[rows, :]
        return tuple(one_head(hh, rows, cos, sin, states[hh]) for hh in range(nh))

    lax.fori_loop(0, n_chunks, body, tuple(jnp.zeros((dh, dh), f32) for _ in range(nh)))


def _retention(p_ret, cos, sin, gains, layer, batch, seq, heads):
    m = p_ret.shape[0]
    dh = gains.shape[-1]
    hp = RET_HEADS_PER_STEP if heads % RET_HEADS_PER_STEP == 0 else 1
    hg = heads // hp
    chunk = _tile(seq, 512)
    blk = lambda seg: pl.BlockSpec((seq, hp * dh), lambda b, g: (b, seg * hg + g))
    tab = pl.BlockSpec((seq, dh // 2), lambda b, g: (0, 0))
    return pl.pallas_call(
        functools.partial(_retention_kernel, chunk=chunk, n_chunks=seq // chunk, dh=dh),
        grid=(batch, hg),
        in_specs=[blk(0), blk(1), blk(2), blk(3), tab, tab,
                  pl.BlockSpec((hp, 1, dh), lambda b, g: (layer * hg + g, 0, 0))],
        out_specs=pl.BlockSpec((seq, hp * dh), lambda b, g: (b, g)),
        out_shape=jax.ShapeDtypeStruct((m, heads * dh), ACT_DTYPE),
        compiler_params=pltpu.CompilerParams(
            dimension_semantics=("arbitrary", "arbitrary"),
            vmem_limit_bytes=_vmem_limit(
                5 * _nbytes((seq, hp * dh), p_ret.dtype) + 2 * _nbytes((seq, dh // 2), jnp.float32),
                12 * hp * _nbytes((chunk, max(chunk, dh)), jnp.float32))),
        name="retention",
    )(p_ret, p_ret, p_ret, p_ret, cos, sin, gains)


def _logf_cumsum_kernel(fg_ref, bias_ref, o_ref, *, chunk, n_chunks):
    f32 = jnp.float32
    row = lax.broadcasted_iota(jnp.int32, (chunk, chunk), 0)
    col = lax.broadcasted_iota(jnp.int32, (chunk, chunk), 1)
    upper = (row <= col).astype(ACT_DTYPE)
    bias = bias_ref[...]

    def body(n, carry):
        rows = pl.ds(pl.multiple_of(n * chunk, chunk), chunk)
        log_f = -_softplus(-(fg_ref[rows, :] + bias))
        p1 = log_f.astype(ACT_DTYPE)
        r1 = log_f - p1.astype(f32)
        p2 = r1.astype(ACT_DTYPE)
        p3 = (r1 - p2.astype(f32)).astype(ACT_DTYPE)
        cs = _dot(p1, upper, _TN) + _dot(p2, upper, _TN) + _dot(p3, upper, _TN)
        cs = cs + carry
        o_ref[:, rows] = cs
        return cs[:, chunk - 1:chunk]

    lax.fori_loop(0, n_chunks, body, jnp.zeros((fg_ref.shape[-1], 1), f32))


def _logf_cumsum(fg, bias, layer, batch, seq):
    lanes = fg.shape[-1]
    chunk = _tile(seq, 512)
    return pl.pallas_call(
        functools.partial(_logf_cumsum_kernel, chunk=chunk, n_chunks=seq // chunk),
        grid=(batch,),
        in_specs=[pl.BlockSpec((seq, lanes), lambda b: (b, 0)),
                  pl.BlockSpec((None, 1, lanes), lambda b: (layer, 0, 0))],
        out_specs=pl.BlockSpec((None, lanes, seq), lambda b: (b, 0, 0)),
        out_shape=jax.ShapeDtypeStruct((batch, lanes, seq), jnp.float32),
        compiler_params=pltpu.CompilerParams(
            dimension_semantics=("arbitrary",),
            vmem_limit_bytes=_vmem_limit(2 * _nbytes((seq, lanes), jnp.float32),
                                         8 * _nbytes((chunk, chunk), jnp.float32))),
        name="fox_logf_cumsum",
    )(fg, bias)


def _head_rms(x, gain):
    return x * lax.rsqrt(jnp.mean(x * x, axis=-1, keepdims=True) + EPS) * gain


FOX_HEADS_PER_STEP = 4
FOX_QUERY_SPLIT = 2


def _fox_kernel(q_ref, k_ref, v_ref, cum_ref, qg_ref, kg_ref, o_ref, kn_ref, va_ref, *, tq, nq, dh, scale):
    f32 = jnp.float32
    qi = pl.program_id(2)
    seq = k_ref.shape[0]
    nh = q_ref.shape[-1] // dh
    tr = tq // FOX_QUERY_SPLIT

    @pl.when(qi == 0)
    def _():
        ones_col = (lax.broadcasted_iota(jnp.int32, (seq, dh), 1) == 0).astype(va_ref.dtype)
        for hh in range(nh):
            cs = slice(hh * dh, (hh + 1) * dh)
            kn_ref[:, cs] = (_head_rms(k_ref[:, cs].astype(f32), kg_ref[...]) * (scale * LOG2_E)).astype(kn_ref.dtype)
            va_ref[:, 2 * hh * dh:(2 * hh + 1) * dh] = v_ref[:, cs]
            va_ref[:, (2 * hh + 1) * dh:(2 * hh + 2) * dh] = ones_col

    causal = (lax.broadcasted_iota(jnp.int32, (tr, tr), 0) >= lax.broadcasted_iota(jnp.int32, (tr, tr), 1))

    def one_chain(hh, r0, lo):
        cs = slice(hh * dh, (hh + 1) * dh)
        vs = slice(2 * hh * dh, (2 * hh + 2) * dh)
        q = _head_rms(q_ref[r0:r0 + tr, cs].astype(f32), qg_ref[...]).astype(ACT_DTYPE)

        def logits(a, b):
            return _dot(q, kn_ref[a:b, cs], _NT) - cum_ref[hh:hh + 1, a:b] * LOG2_E

        s_diag = jnp.where(causal, logits(lo, lo + tr), -jnp.inf)
        m = jnp.max(s_diag, axis=-1, keepdims=True)
        if lo:
            s_past = logits(0, lo)
            m = jnp.maximum(m, jnp.max(s_past, axis=-1, keepdims=True))
        acc = _dot(jnp.exp2(s_diag - m).astype(ACT_DTYPE), va_ref[lo:lo + tr, vs])
        if lo:
            acc = acc + _dot(jnp.exp2(s_past - m).astype(ACT_DTYPE), va_ref[0:lo, vs])
        o_ref[r0:r0 + tr, cs] = (acc[:, :dh] / acc[:, dh:dh + 1]).astype(o_ref.dtype)

    for case in range(nq):
        @pl.when(qi == case)
        def _(case=case):
            for hh in range(nh):
                for r0 in range(0, tq, tr):
                    one_chain(hh, r0, case * tq + r0)


def _fox(p, col0, cum, q_gain, k_gain, layer, batch, seq, heads):
    m = p.shape[0]
    dh = q_gain.shape[-1]
    hp = FOX_HEADS_PER_STEP if heads % FOX_HEADS_PER_STEP == 0 else 1
    hg = heads // hp
    assert col0 % (hp * dh) == 0
    g0 = col0 // (hp * dh)
    tq = _tile(seq, 512)
    nq = seq // tq
    cum = cum.reshape(batch * hg, hp, seq)
    gain = pl.BlockSpec((None, 1, dh), lambda b, g, i: (layer, 0, 0))
    return pl.pallas_call(
        functools.partial(_fox_kernel, tq=tq, nq=nq, dh=dh, scale=dh ** -0.5),
        grid=(batch, hg, nq),
        in_specs=[pl.BlockSpec((tq, hp * dh), lambda b, g, i: (b * nq + i, g0 + g)),
                  pl.BlockSpec((seq, hp * dh), lambda b, g, i: (b, g0 + hg + g)),
                  pl.BlockSpec((seq, hp * dh), lambda b, g, i: (b, g0 + 2 * hg + g)),
                  pl.BlockSpec((None, hp, seq), lambda b, g, i: (b * hg + g, 0, 0)),
                  gain, gain],
        out_specs=pl.BlockSpec((tq, hp * dh), lambda b, g, i: (b * nq + i, g)),
        out_shape=jax.ShapeDtypeStruct((m, heads * dh), ACT_DTYPE),
        scratch_shapes=[pltpu.VMEM((seq, hp * dh), ACT_DTYPE), pltpu.VMEM((seq, 2 * hp * dh), ACT_DTYPE)],
        compiler_params=pltpu.CompilerParams(
            dimension_semantics=("arbitrary", "arbitrary", "arbitrary"),
            vmem_limit_bytes=_vmem_limit(3 * _nbytes((seq, hp * dh), p.dtype),
                                         3 * _nbytes((seq, hp * dh), ACT_DTYPE) + 4 * hp * _nbytes((tq, seq), jnp.float32))),
        name="forgetting_attention",
    )(p, p, p, cum, q_gain, k_gain)


def _logf_cumsum_rows_kernel(fg_ref, bias_ref, o_ref, *, chunk, n_chunks):
    f32 = jnp.float32
    row = lax.broadcasted_iota(jnp.int32, (chunk, chunk), 0)
    col = lax.broadcasted_iota(jnp.int32, (chunk, chunk), 1)
    lower = (row >= col).astype(ACT_DTYPE)
    bias = bias_ref[...]
    carry = jnp.zeros((1, fg_ref.shape[-1]), f32)
    for n in range(n_chunks):
        rows = slice(n * chunk, (n + 1) * chunk)
        log_f = -_softplus(-(fg_ref[rows, :] + bias))
        p1 = log_f.astype(ACT_DTYPE)
        r1 = log_f - p1.astype(f32)
        p2 = r1.astype(ACT_DTYPE)
        p3 = (r1 - p2.astype(f32)).astype(ACT_DTYPE)
        cs = _dot(lower, p1) + _dot(lower, p2) + _dot(lower, p3) + carry
        o_ref[rows, :] = cs
        carry = cs[chunk - 1:chunk]


def _logf_cumsum_rows(fg, bias, layer, batch, seq):
    lanes = fg.shape[-1]
    chunk = _tile(seq, 512)
    return pl.pallas_call(
        functools.partial(_logf_cumsum_rows_kernel, chunk=chunk, n_chunks=seq // chunk),
        grid=(batch,),
        in_specs=[pl.BlockSpec((seq, lanes), lambda b: (b, 0)),
                  pl.BlockSpec((None, 1, lanes), lambda b: (layer, 0, 0))],
        out_specs=pl.BlockSpec((seq, lanes), lambda b: (b, 0)),
        out_shape=jax.ShapeDtypeStruct(fg.shape, jnp.float32),
        compiler_params=pltpu.CompilerParams(
            dimension_semantics=("arbitrary",),
            vmem_limit_bytes=_vmem_limit(2 * _nbytes((seq, lanes), jnp.float32),
                                         8 * _nbytes((chunk, chunk), jnp.float32))),
        name="fox_logf_cumsum",
    )(fg, bias)


def _fox_t_kernel(q_ref, k_ref, v_ref, cum_ref, qg_ref, kg_ref, o_ref, kn_ref, vt_ref, cb_ref, *, tq, nq, dh, scale):
    f32 = jnp.float32
    qi = pl.program_id(2)
    seq = k_ref.shape[0]
    nh = q_ref.shape[-1] // dh
    tr = tq // FOX_QUERY_SPLIT

    @pl.when(qi == 0)
    def _():
        lane = lax.broadcasted_iota(jnp.int32, (seq, V7X_LANES), 1)
        for hh in range(nh):
            cs = slice(hh * dh, (hh + 1) * dh)
            kn_ref[:, cs] = (_head_rms(k_ref[:, cs].astype(f32), kg_ref[...]) * (scale * LOG2_E)).astype(kn_ref.dtype)
            vt_ref[cs, :] = v_ref[:, cs].astype(f32).T.astype(vt_ref.dtype)
            head = pl.program_id(1) * nh + hh
            col = jnp.sum(jnp.where(lane == head, cum_ref[...], 0.0), axis=-1, keepdims=True)
            cb_ref[hh] = jnp.broadcast_to(col * LOG2_E, (seq, V7X_LANES))

    key_le_query = (lax.broadcasted_iota(jnp.int32, (tr, tr), 0) <= lax.broadcasted_iota(jnp.int32, (tr, tr), 1))

    def one_chain(hh, r0, lo):
        cs = slice(hh * dh, (hh + 1) * dh)
        q = _head_rms(q_ref[r0:r0 + tr, cs].astype(f32), qg_ref[...]).astype(ACT_DTYPE)

        def logits_t(a, b):
            bias = cb_ref[hh, a:b, :]
            return _dot(kn_ref[a:b, cs], q, _NT) - jnp.concatenate([bias] * (tr // V7X_LANES), axis=-1)

        s_diag = jnp.where(key_le_query, logits_t(lo, lo + tr), -jnp.inf)
        m = jnp.max(s_diag, axis=0, keepdims=True)
        if lo:
            s_past = logits_t(0, lo)
            m = jnp.maximum(m, jnp.max(s_past, axis=0, keepdims=True))
        p = jnp.exp2(s_diag - m)
        l = jnp.sum(p, axis=0, keepdims=True)
        acc = _dot(vt_ref[cs, lo:lo + tr], p.astype(ACT_DTYPE))
        if lo:
            p = jnp.exp2(s_past - m)
            l = l + jnp.sum(p, axis=0, keepdims=True)
            acc = acc + _dot(vt_ref[cs, 0:lo], p.astype(ACT_DTYPE))
        o_ref[r0:r0 + tr, cs] = (acc / l).T.astype(o_ref.dtype)

    for case in range(nq):
        @pl.when(qi == case)
        def _(case=case):
            for hh in range(nh):
                for r0 in range(0, tq, tr):
                    one_chain(hh, r0, case * tq + r0)


def _fox_t(p, col0, cum, q_gain, k_gain, layer, batch, seq, heads):
    m = p.shape[0]
    dh = q_gain.shape[-1]
    hp = FOX_HEADS_PER_STEP if heads % FOX_HEADS_PER_STEP == 0 else 1
    hg = heads // hp
    assert col0 % (hp * dh) == 0
    g0 = col0 // (hp * dh)
    tq = _tile(seq, 512)
    nq = seq // tq
    gain = pl.BlockSpec((None, 1, dh), lambda b, g, i: (layer, 0, 0))
    return pl.pallas_call(
        functools.partial(_fox_t_kernel, tq=tq, nq=nq, dh=dh, scale=dh ** -0.5),
        grid=(batch, hg, nq),
        in_specs=[pl.BlockSpec((tq, hp * dh), lambda b, g, i: (b * nq + i, g0 + g)),
                  pl.BlockSpec((seq, hp * dh), lambda b, g, i: (b, g0 + hg + g)),
                  pl.BlockSpec((seq, hp * dh), lambda b, g, i: (b, g0 + 2 * hg + g)),
                  pl.BlockSpec((seq, V7X_LANES), lambda b, g, i: (b, 0)),
                  gain, gain],
        out_specs=pl.BlockSpec((tq, hp * dh), lambda b, g, i: (b * nq + i, g0 * 0 + g)),
        out_shape=jax.ShapeDtypeStruct((m, heads * dh), ACT_DTYPE),
        scratch_shapes=[pltpu.VMEM((seq, hp * dh), ACT_DTYPE), pltpu.VMEM((hp * dh, seq), ACT_DTYPE),
                        pltpu.VMEM((hp, seq, V7X_LANES), jnp.float32)],
        compiler_params=pltpu.CompilerParams(
            dimension_semantics=("arbitrary", "arbitrary", "arbitrary"),
            vmem_limit_bytes=_vmem_limit(3 * _nbytes((seq, hp * dh), p.dtype) + _nbytes((seq, V7X_LANES), jnp.float32),
                                         2 * _nbytes((seq, hp * dh), ACT_DTYPE) + hp * _nbytes((seq, V7X_LANES), jnp.float32)
                                         + 4 * hp * _nbytes((tq, seq), jnp.float32))),
        name="forgetting_attention",
    )(p, p, p, cum, q_gain, k_gain)


LRU_PAD_ROWS = 8
LRU_CHUNK_ROWS = 32


def _lru_kernel(gate_ref, x_ref, cw_ref, prm_ref, wa_ref, wx_ref, o_ref, xs_ref, us_ref, as_ref, *, conv_width):
    f32 = jnp.float32
    seq, width = x_ref.shape
    pad = LRU_PAD_ROWS
    assert conv_width - 1 <= pad
    body = pl.ds(pad, seq)

    def shifted(ref, s):
        return ref[pl.ds(pad - s, seq), :]

    x = x_ref[...].astype(f32)
    xs_ref[0:pad, :] = jnp.zeros((pad, width), f32)
    xs_ref[body, :] = x
    conv_b, b_a, b_x, lam = (prm_ref[i:i + 1, :] for i in range(4))
    xc = conv_b + x * cw_ref[conv_width - 1:conv_width, :]
    for s in range(1, conv_width):
        xc = xc + shifted(xs_ref, s) * cw_ref[conv_width - 1 - s:conv_width - s, :]
    xcb = xc.astype(ACT_DTYPE)
    r = _sigmoid(_dot(xcb, wa_ref[...]) + b_a)
    i = _sigmoid(_dot(xcb, wx_ref[...]) + b_x)
    log_a = -LRU_C * r * _softplus(-lam)
    a = jnp.exp(log_a)
    w = -jnp.tanh(log_a) * (a * a + 1.0)
    u = jnp.where(w > 0.0, w * lax.rsqrt(w), 0.0) * (i * xc)
    chunk = min(LRU_CHUNK_ROWS, seq)
    h_chunks, carry = [], None
    for c0 in range(0, seq, chunk):
        uc, ac = u[c0:c0 + chunk], a[c0:c0 + chunk]
        base = (c0 // chunk) * (pad + chunk)
        rows = pl.ds(base + pad, chunk)
        us_ref[base:base + pad, :] = jnp.zeros((pad, width), f32)
        as_ref[base:base + pad, :] = jnp.ones((pad, width), f32)
        s = 1
        while s < min(pad, chunk):
            us_ref[rows, :] = uc
            as_ref[rows, :] = ac
            uc = ac * us_ref[pl.ds(base + pad - s, chunk), :] + uc
            ac = ac * as_ref[pl.ds(base + pad - s, chunk), :]
            s *= 2
        while s < chunk:
            uc = jnp.concatenate([uc[:s], ac[s:] * uc[:-s] + uc[s:]], axis=0)
            ac = jnp.concatenate([ac[:s], ac[s:] * ac[:-s]], axis=0)
            s *= 2
        if carry is not None:
            uc = ac * carry + uc
        carry = uc[chunk - 1:chunk]
        h_chunks.append(uc)
    h = jnp.concatenate(h_chunks, axis=0)
    o_ref[...] = (jax.nn.gelu(gate_ref[...].astype(f32)) * h).astype(o_ref.dtype)


def _lru(p_lru, conv_w, prm, w_a, w_x, layer, batch, seq):
    m = p_lru.shape[0]
    nblk, bw = w_a.shape[1], w_a.shape[2]
    conv_width = conv_w.shape[1]
    chunk = min(LRU_CHUNK_ROWS, seq)
    wmat = pl.BlockSpec((None, None, bw, bw), lambda b, c: (layer, c, 0, 0))
    return pl.pallas_call(
        functools.partial(_lru_kernel, conv_width=conv_width),
        grid=(batch, nblk),
        in_specs=[pl.BlockSpec((seq, bw), lambda b, c: (b, c)),
                  pl.BlockSpec((seq, bw), lambda b, c: (b, nblk + c)),
                  pl.BlockSpec((None, conv_width, bw), lambda b, c: (layer, 0, c)),
                  pl.BlockSpec((None, 8, bw), lambda b, c: (layer, 0, c)),
                  wmat, wmat],
        out_specs=pl.BlockSpec((seq, bw), lambda b, c: (b, c)),
        out_shape=jax.ShapeDtypeStruct((m, nblk * bw), ACT_DTYPE),
        scratch_shapes=[pltpu.VMEM((LRU_PAD_ROWS + seq, bw), jnp.float32)]
        + [pltpu.VMEM((seq // chunk * (LRU_PAD_ROWS + chunk), bw), jnp.float32)] * 2,
        compiler_params=pltpu.CompilerParams(
            dimension_semantics=("arbitrary", "arbitrary"),
            vmem_limit_bytes=_vmem_limit(3 * _nbytes((seq, bw), p_lru.dtype),
                                         16 * _nbytes((seq, bw), jnp.float32))),
        name="rg_lru",
    )(p_lru, p_lru, conv_w, prm, w_a, w_x)


def kernel(x, ffn1_norm, ffn1_w_gate, ffn1_w_up, ffn1_w_down, mix_norm, w_in, ret_norm, fox_q_norm, fox_k_norm, fox_f_bias, lru_conv_w, lru_conv_b, lru_w_a, lru_b_a, lru_w_x, lru_b_x, lru_lambda, w_branch_ret, w_branch_fox, w_branch_lru, w_out, ffn2_norm, ffn2_w_gate, ffn2_w_up, ffn2_w_down):
    batch, seq, d = x.shape
    depth = w_in.shape[0]
    ret_heads, ret_dh = ret_norm.shape[1], ret_norm.shape[2]
    fox_heads, fox_dh = fox_f_bias.shape[1], fox_q_norm.shape[1]
    ret_w, fox_w, lru_w = ret_heads * ret_dh, fox_heads * fox_dh, lru_lambda.shape[1]
    assert fox_heads <= V7X_LANES
    c_fox = 4 * ret_w
    c_fg = c_fox + 3 * fox_w
    c_tail = c_fg + fox_heads
    assert w_in.shape[-1] == c_tail + 2 * lru_w + N_BRANCH * d

    f32 = jnp.float32
    as_rows = lambda p: p.reshape(depth, 1, -1)
    ffn1_g, mix_g, ffn2_g = as_rows(ffn1_norm), as_rows(mix_norm), as_rows(ffn2_norm)
    fox_qg, fox_kg = as_rows(fox_q_norm), as_rows(fox_k_norm)
    ret_g = ret_norm.reshape(depth * ret_heads, 1, ret_dh)
    fg_bias = as_rows(jnp.pad(fox_f_bias, ((0, 0), (0, V7X_LANES - fox_heads))))
    zeros = jnp.zeros_like(lru_lambda)
    lru_prm = jnp.stack([lru_conv_b, lru_b_a, lru_b_x, lru_lambda, zeros, zeros, zeros, zeros], axis=1)

    w_in_t = jnp.swapaxes(w_in, 1, 2)
    assert c_fg % 8 == 0 and c_tail % 8 == 0 and c_fg + V7X_LANES <= w_in.shape[-1]

    half = ret_dh // 2
    inv_freq = ROPE_BASE ** (-jnp.arange(half, dtype=f32) / half)
    ang = jnp.arange(seq).astype(f32)[:, None] * inv_freq[None, :]
    cos, sin = jnp.cos(ang), jnp.sin(ang)

    x = x.reshape(batch * seq, d)
    xg, r = _prep(x, ffn1_g, 0)
    for l in range(depth):
        h = _gateup(xg, r, ffn1_w_gate, ffn1_w_up, l)
        x, xg, r = _mm_resid(h, ffn1_w_down, x, l, HALF_STEP, "ffn_down", next_gain=(mix_g, l))

        proj = functools.partial(_mm_nt, xg, r, w_in_t, layer=l)
        p_head = proj(row0=0, n=c_fg, out_dtype=ACT_DTYPE, name="proj_ret_fox")
        p_tail = proj(row0=c_tail, n=2 * lru_w + N_BRANCH * d, out_dtype=ACT_DTYPE, name="proj_lru_gates")
        fg = proj(row0=c_fg, n=V7X_LANES, out_dtype=f32, name="proj_fgate")

        ret = _retention(p_head, cos, sin, ret_g, l, batch, seq, ret_heads)
        cum = _logf_cumsum_rows(fg, fg_bias, l, batch, seq)
        fox = _fox_t(p_head, c_fox, cum, fox_qg, fox_kg, l, batch, seq, fox_heads)
        lru = _lru(p_tail, lru_conv_w, lru_prm, lru_w_a, lru_w_x, l, batch, seq)

        mixed = _merge(ret, fox, lru, w_branch_ret, w_branch_fox, w_branch_lru, p_tail, 2 * lru_w, l)
        x, xg, r = _mm_resid(mixed, w_out, x, l, 1.0, "out_proj", next_gain=(ffn2_g, l))

        h = _gateup(xg, r, ffn2_w_gate, ffn2_w_up, l)
        if l + 1 < depth:
            x, xg, r = _mm_resid(h, ffn2_w_down, x, l, HALF_STEP, "ffn_down", next_gain=(ffn1_g, l + 1))
        else:
            x = _mm_resid(h, ffn2_w_down, x, l, HALF_STEP, "ffn_down")
    return x.reshape(batch, seq, d)
```

```python
import functools
import math

import jax
import jax.numpy as jnp
from jax import lax
from jax.experimental import pallas as pl
from jax.experimental.pallas import tpu as pltpu

EPS = 1e-6
HALF_STEP = 0.5
ROPE_BASE = 10000.0
LRU_C = 8.0
N_BRANCH = 3
LOG2_E = 1.4426950408889634

V7X_LANES = 128
V7X_VMEM_REQUEST_CAP = 62 * 1024 * 1024
ACT_DTYPE = jnp.bfloat16


def _vmem_limit(pipelined_bytes, resident_bytes=0):
    need = 2 * pipelined_bytes + resident_bytes
    return int(min(max(need, 16 * 1024 * 1024), V7X_VMEM_REQUEST_CAP))


def _nbytes(shape, dtype):
    return math.prod(shape) * jnp.dtype(dtype).itemsize


def _tile(n, pref):
    t = min(n, pref)
    assert n % t == 0, (n, pref)
    return t


def _dot(a, b, dims=(((1,), (0,)), ((), ()))):
    return lax.dot_general(a, b, dims, preferred_element_type=jnp.float32)


_NT = (((1,), (1,)), ((), ()))
_TN = (((0,), (0,)), ((), ()))


def _sigmoid(x):
    return 1.0 / (1.0 + jnp.exp(-x))


def _softplus(x):
    return jnp.maximum(x, 0.0) + jnp.log1p(jnp.exp(-jnp.abs(x)))


def _lane_partial(x):
    parts = [x[:, c:c + V7X_LANES] for c in range(0, x.shape[-1], V7X_LANES)]
    return functools.reduce(lambda p, q: p + q, parts)


def _row_scale(r_ref, width):
    r = r_ref[...]
    return r if width == V7X_LANES else jnp.concatenate([r] * (width // V7X_LANES), axis=-1)


def _emit_norm(x_new, gain_ref, xg_ref, r_ref, j, nj, d):
    xg_ref[...] = (x_new * gain_ref[...]).astype(xg_ref.dtype)
    r_ref[...] += _lane_partial(x_new * x_new)

    @pl.when(j == nj - 1)
    def _():
        ssq = jnp.sum(r_ref[...], axis=-1, keepdims=True)
        r_ref[...] = jnp.broadcast_to(lax.rsqrt(ssq * (1.0 / d) + EPS), r_ref.shape)


def _prep_kernel(x_ref, g_ref, xg_ref, r_ref):
    x = x_ref[...]
    xg_ref[...] = (x * g_ref[...]).astype(xg_ref.dtype)
    ms = jnp.mean(x * x, axis=-1, keepdims=True)
    r_ref[...] = jnp.broadcast_to(lax.rsqrt(ms + EPS), r_ref.shape)


def _prep(x, gains, layer):
    m, d = x.shape
    tr = _tile(m, 256)
    return pl.pallas_call(
        _prep_kernel,
        grid=(m // tr,),
        in_specs=[pl.BlockSpec((tr, d), lambda i: (i, 0)),
                  pl.BlockSpec((None, 1, d), lambda i: (layer, 0, 0))],
        out_specs=[pl.BlockSpec((tr, d), lambda i: (i, 0)),
                   pl.BlockSpec((tr, V7X_LANES), lambda i: (i, 0))],
        out_shape=[jax.ShapeDtypeStruct((m, d), ACT_DTYPE),
                   jax.ShapeDtypeStruct((m, V7X_LANES), jnp.float32)],
        compiler_params=pltpu.CompilerParams(
            dimension_semantics=("arbitrary",),
            vmem_limit_bytes=_vmem_limit(_nbytes((tr, d), jnp.float32) + _nbytes((tr, d), ACT_DTYPE),
                                         2 * _nbytes((tr, d), jnp.float32))),
        name="input_norm",
    )(x, gains)


def _gateup_kernel(a_ref, r_ref, wg_ref, wu_ref, o_ref):
    a = a_ref[...]
    r = _row_scale(r_ref, o_ref.shape[-1])
    g = r * _dot(a, wg_ref[...])
    u = r * _dot(a, wu_ref[...])
    o_ref[...] = (g * _sigmoid(g) * u).astype(o_ref.dtype)


def _gateup(xg, r, w_gate, w_up, layer):
    m, k = xg.shape
    f = w_gate.shape[-1]
    tm, tf = _tile(m, 2048), _tile(f, 256)
    w_spec = pl.BlockSpec((None, k, tf), lambda i, j: (layer, 0, j))
    return pl.pallas_call(
        _gateup_kernel,
        grid=(m // tm, f // tf),
        in_specs=[pl.BlockSpec((tm, k), lambda i, j: (i, 0)),
                  pl.BlockSpec((tm, V7X_LANES), lambda i, j: (i, 0)), w_spec, w_spec],
        out_specs=pl.BlockSpec((tm, tf), lambda i, j: (i, j)),
        out_shape=jax.ShapeDtypeStruct((m, f), ACT_DTYPE),
        compiler_params=pltpu.CompilerParams(
            dimension_semantics=("arbitrary", "arbitrary"),
            vmem_limit_bytes=_vmem_limit(
                _nbytes((tm, k), xg.dtype) + 2 * _nbytes((k, tf), w_gate.dtype) + _nbytes((tm, tf), ACT_DTYPE),
                4 * _nbytes((tm, tf), jnp.float32))),
        name="ffn_gateup",
    )(xg, r, w_gate, w_up)


def _mm_resid_kernel(a_ref, w_ref, x_ref, *rest, scale, nj, d, emit):
    if emit:
        gain_ref, o_ref, xg_ref, r_ref = rest
        j = pl.program_id(1)

        @pl.when(j == 0)
        def _():
            r_ref[...] = jnp.zeros(r_ref.shape, r_ref.dtype)
    else:
        (o_ref,) = rest
    x_new = x_ref[...] + scale * _dot(a_ref[...], w_ref[...])
    o_ref[...] = x_new
    if emit:
        _emit_norm(x_new, gain_ref, xg_ref, r_ref, j, nj, d)


def _mm_resid(a, w, x, layer, scale, name, next_gain=None):
    m, k = a.shape
    n = w.shape[-1]
    tm, tn = _tile(m, 1024), _tile(n, 512 if k <= 4096 else 256)
    nj = n // tn
    emit = next_gain is not None
    tile = pl.BlockSpec((tm, tn), lambda i, j: (i, j))
    in_specs = [pl.BlockSpec((tm, k), lambda i, j: (i, 0)),
                pl.BlockSpec((None, k, tn), lambda i, j: (layer, 0, j)), tile]
    out_specs, out_shape, args = tile, jax.ShapeDtypeStruct((m, n), jnp.float32), [a, w, x]
    if emit:
        gains, gl = next_gain
        in_specs.append(pl.BlockSpec((None, 1, tn), lambda i, j: (gl, 0, j)))
        args.append(gains)
        out_specs = [tile, tile, pl.BlockSpec((tm, V7X_LANES), lambda i, j: (i, 0))]
        out_shape = [out_shape, jax.ShapeDtypeStruct((m, n), ACT_DTYPE),
                     jax.ShapeDtypeStruct((m, V7X_LANES), jnp.float32)]
    return pl.pallas_call(
        functools.partial(_mm_resid_kernel, scale=scale, nj=nj, d=n, emit=emit),
        grid=(m // tm, nj),
        in_specs=in_specs, out_specs=out_specs, out_shape=out_shape,
        compiler_params=pltpu.CompilerParams(
            dimension_semantics=("arbitrary", "arbitrary"),
            vmem_limit_bytes=_vmem_limit(
                _nbytes((tm, k), a.dtype) + _nbytes((k, tn), w.dtype) + 3 * _nbytes((tm, tn), jnp.float32),
                2 * _nbytes((tm, tn), jnp.float32))),
        name=name,
    )(*args)


def _mm_nt_kernel(a_ref, r_ref, wt_ref, o_ref):
    y = _row_scale(r_ref, o_ref.shape[-1]) * _dot(a_ref[...], wt_ref[0], _NT)
    o_ref[...] = y.astype(o_ref.dtype)


def _mm_nt(xg, r, wt, *, layer, row0, n, out_dtype, name):
    m, k = xg.shape
    tm, tn = _tile(m, 2048), _tile(n, 512)
    return pl.pallas_call(
        _mm_nt_kernel,
        grid=(m // tm, n // tn),
        in_specs=[pl.BlockSpec((tm, k), lambda i, j: (i, 0)),
                  pl.BlockSpec((tm, V7X_LANES), lambda i, j: (i, 0)),
                  pl.BlockSpec((pl.Element(1), pl.Element(tn), pl.Element(k)),
                               lambda i, j: (layer, pl.multiple_of(row0 + j * tn, 8), 0))],
        out_specs=pl.BlockSpec((tm, tn), lambda i, j: (i, j)),
        out_shape=jax.ShapeDtypeStruct((m, n), out_dtype),
        compiler_params=pltpu.CompilerParams(
            dimension_semantics=("arbitrary", "arbitrary"),
            vmem_limit_bytes=_vmem_limit(
                _nbytes((tm, k), xg.dtype) + _nbytes((tn, k), wt.dtype) + _nbytes((tm, tn), out_dtype),
                2 * _nbytes((tm, tn), jnp.float32))),
        name=name,
    )(xg, r, wt)


def _merge_kernel(r_ref, f_ref, l_ref, wr_ref, wf_ref, wl_ref, gr_ref, gf_ref, gl_ref, o_ref):
    gate = lambda g_ref: _sigmoid(g_ref[...].astype(jnp.float32))
    y = gate(gr_ref) * _dot(r_ref[...], wr_ref[...])
    y = y + gate(gf_ref) * _dot(f_ref[...], wf_ref[...])
    y = y + gate(gl_ref) * _dot(l_ref[...], wl_ref[...])
    o_ref[...] = y.astype(o_ref.dtype)


def _merge(ret, fox, lru, w_ret, w_fox, w_lru, gates, gate_col0, layer):
    m, kw = ret.shape
    d = w_ret.shape[-1]
    tm, tn = _tile(m, 1024), _tile(d, 512)
    nj = d // tn
    assert gate_col0 % tn == 0
    a_spec = pl.BlockSpec((tm, kw), lambda j, i: (i, 0))
    w_spec = pl.BlockSpec((None, kw, tn), lambda j, i: (layer, 0, j))
    g_specs = [pl.BlockSpec((tm, tn), functools.partial(lambda j, i, o: (i, o + j), o=gate_col0 // tn + b * nj))
               for b in range(N_BRANCH)]
    return pl.pallas_call(
        _merge_kernel,
        grid=(nj, m // tm),
        in_specs=[a_spec, a_spec, a_spec, w_spec, w_spec, w_spec] + g_specs,
        out_specs=pl.BlockSpec((tm, tn), lambda j, i: (i, j)),
        out_shape=jax.ShapeDtypeStruct((m, d), ACT_DTYPE),
        compiler_params=pltpu.CompilerParams(
            dimension_semantics=("arbitrary", "arbitrary"),
            vmem_limit_bytes=_vmem_limit(
                3 * _nbytes((tm, kw), ret.dtype) + 3 * _nbytes((kw, tn), w_ret.dtype) + 4 * _nbytes((tm, tn), ACT_DTYPE),
                4 * _nbytes((tm, tn), jnp.float32))),
        name="branch_merge",
    )(ret, fox, lru, w_ret, w_fox, w_lru, gates, gates, gates)


RET_HEADS_PER_STEP = 2


def _retention_kernel(q_ref, k_ref, v_ref, g_ref, cos_ref, sin_ref, gain_ref, o_ref, *, chunk, n_chunks, dh):
    half = dh // 2
    f32 = jnp.float32
    nh = q_ref.shape[-1] // dh
    row = lax.broadcasted_iota(jnp.int32, (chunk, chunk), 0)
    col = lax.broadcasted_iota(jnp.int32, (chunk, chunk), 1)
    rel = (row - col).astype(f32)
    idx = lax.broadcasted_iota(jnp.int32, (chunk, 1), 0).astype(f32)

    def decays(hh):
        h = (pl.program_id(1) * nh + hh).astype(f32)
        log_gamma = jnp.log(1.0 - jnp.exp2(jnp.full((1, 1), -5.0, f32) - h))
        inner_decay = jnp.where(rel >= 0, jnp.exp(jnp.maximum(rel, 0.0) * log_gamma), 0.0)
        k_to_end = jnp.exp((chunk - 1.0 - idx) * log_gamma)
        q_from_start = jnp.exp((idx + 1.0) * log_gamma)
        chunk_decay = jnp.exp(chunk * log_gamma)
        return inner_decay, k_to_end, q_from_start, chunk_decay

    tables = [decays(hh) for hh in range(nh)]

    def rotary(x, cos, sin):
        x1, x2 = x[:, :half], x[:, half:]
        return jnp.concatenate([x1 * cos - x2 * sin, x2 * cos + x1 * sin], axis=-1)

    def one_head(hh, rows, cos, sin, state):
        inner_decay, k_to_end, q_from_start, chunk_decay = tables[hh]
        cs = slice(hh * dh, (hh + 1) * dh)
        q = rotary(q_ref[rows, cs].astype(f32), cos, sin)
        k = rotary(k_ref[rows, cs].astype(f32), cos, sin) * dh ** -0.5
        v = v_ref[rows, cs]
        scores = _dot(q.astype(ACT_DTYPE), k.astype(ACT_DTYPE), _NT) * inner_decay
        inner = _dot(scores.astype(ACT_DTYPE), v)
        cross = _dot((q * q_from_start).astype(ACT_DTYPE), state.astype(ACT_DTYPE))
        kv = _dot((k * k_to_end).astype(ACT_DTYPE), v, _TN)
        o = inner + cross
        o = o * lax.rsqrt(jnp.mean(o * o, axis=-1, keepdims=True) + EPS) * gain_ref[hh]
        g = g_ref[rows, cs].astype(f32)
        o_ref[rows, cs] = (g * _sigmoid(g) * o).astype(o_ref.dtype)
        return chunk_decay * state + kv

    def body(n, states):
        rows = pl.ds(pl.multiple_of(n * chunk, chunk), chunk)
        cos, sin = cos_ref[rows, :], sin_ref[rows, :]
        return tuple(one_head(hh, rows, cos, sin, states[hh]) for hh in range(nh))

    lax.fori_loop(0, n_chunks, body, tuple(jnp.zeros((dh, dh), f32) for _ in range(nh)))


def _retention(p_ret, cos, sin, gains, layer, batch, seq, heads):
    m = p_ret.shape[0]
    dh = gains.shape[-1]
    hp = RET_HEADS_PER_STEP if heads % RET_HEADS_PER_STEP == 0 else 1
    hg = heads // hp
    chunk = _tile(seq, 256)
    blk = lambda seg: pl.BlockSpec((seq, hp * dh), lambda b, g: (b, seg * hg + g))
    tab = pl.BlockSpec((seq, dh // 2), lambda b, g: (0, 0))
    return pl.pallas_call(
        functools.partial(_retention_kernel, chunk=chunk, n_chunks=seq // chunk, dh=dh),
        grid=(batch, hg),
        in_specs=[blk(0), blk(1), blk(2), blk(3), tab, tab,
                  pl.BlockSpec((hp, 1, dh), lambda b, g: (layer * hg + g, 0, 0))],
        out_specs=pl.BlockSpec((seq, hp * dh), lambda b, g: (b, g)),
        out_shape=jax.ShapeDtypeStruct((m, heads * dh), ACT_DTYPE),
        compiler_params=pltpu.CompilerParams(
            dimension_semantics=("arbitrary", "arbitrary"),
            vmem_limit_bytes=_vmem_limit(
                5 * _nbytes((seq, hp * dh), p_ret.dtype) + 2 * _nbytes((seq, dh // 2), jnp.float32),
                12 * hp * _nbytes((chunk, max(chunk, dh)), jnp.float32))),
        name="retention",
    )(p_ret, p_ret, p_ret, p_ret, cos, sin, gains)


def _logf_cumsum_kernel(fg_ref, bias_ref, o_ref, *, chunk, n_chunks):
    f32 = jnp.float32
    row = lax.broadcasted_iota(jnp.int32, (chunk, chunk), 0)
    col = lax.broadcasted_iota(jnp.int32, (chunk, chunk), 1)
    upper = (row <= col).astype(ACT_DTYPE)
    bias = bias_ref[...]

    def body(n, carry):
        rows = pl.ds(pl.multiple_of(n * chunk, chunk), chunk)
        log_f = -_softplus(-(fg_ref[rows, :] + bias))
        p1 = log_f.astype(ACT_DTYPE)
        r1 = log_f - p1.astype(f32)
        p2 = r1.astype(ACT_DTYPE)
        p3 = (r1 - p2.astype(f32)).astype(ACT_DTYPE)
        cs = _dot(p1, upper, _TN) + _dot(p2, upper, _TN) + _dot(p3, upper, _TN)
        cs = cs + carry
        o_ref[:, rows] = cs
        return cs[:, chunk - 1:chunk]

    lax.fori_loop(0, n_chunks, body, jnp.zeros((fg_ref.shape[-1], 1), f32))


def _logf_cumsum(fg, bias, layer, batch, seq):
    lanes = fg.shape[-1]
    chunk = _tile(seq, 512)
    return pl.pallas_call(
        functools.partial(_logf_cumsum_kernel, chunk=chunk, n_chunks=seq // chunk),
        grid=(batch,),
        in_specs=[pl.BlockSpec((seq, lanes), lambda b: (b, 0)),
                  pl.BlockSpec((None, 1, lanes), lambda b: (layer, 0, 0))],
        out_specs=pl.BlockSpec((None, lanes, seq), lambda b: (b, 0, 0)),
        out_shape=jax.ShapeDtypeStruct((batch, lanes, seq), jnp.float32),
        compiler_params=pltpu.CompilerParams(
            dimension_semantics=("arbitrary",),
            vmem_limit_bytes=_vmem_limit(2 * _nbytes((seq, lanes), jnp.float32),
                                         8 * _nbytes((chunk, chunk), jnp.float32))),
        name="fox_logf_cumsum",
    )(fg, bias)


def _head_rms(x, gain):
    return x * lax.rsqrt(jnp.mean(x * x, axis=-1, keepdims=True) + EPS) * gain


FOX_HEADS_PER_STEP = 4
FOX_QUERY_SPLIT = 2


def _fox_kernel(q_ref, k_ref, v_ref, cum_ref, qg_ref, kg_ref, o_ref, kn_ref, va_ref, *, tq, nq, dh, scale):
    f32 = jnp.float32
    qi = pl.program_id(2)
    seq = k_ref.shape[0]
    nh = q_ref.shape[-1] // dh
    tr = tq // FOX_QUERY_SPLIT

    @pl.when(qi == 0)
    def _():
        ones_col = (lax.broadcasted_iota(jnp.int32, (seq, dh), 1) == 0).astype(va_ref.dtype)
        for hh in range(nh):
            cs = slice(hh * dh, (hh + 1) * dh)
            kn_ref[:, cs] = (_head_rms(k_ref[:, cs].astype(f32), kg_ref[...]) * (scale * LOG2_E)).astype(kn_ref.dtype)
            va_ref[:, 2 * hh * dh:(2 * hh + 1) * dh] = v_ref[:, cs]
            va_ref[:, (2 * hh + 1) * dh:(2 * hh + 2) * dh] = ones_col

    causal = (lax.broadcasted_iota(jnp.int32, (tr, tr), 0) >= lax.broadcasted_iota(jnp.int32, (tr, tr), 1))

    def one_chain(hh, r0, lo):
        cs = slice(hh * dh, (hh + 1) * dh)
        vs = slice(2 * hh * dh, (2 * hh + 2) * dh)
        q = _head_rms(q_ref[r0:r0 + tr, cs].astype(f32), qg_ref[...]).astype(ACT_DTYPE)

        def logits(a, b):
            return _dot(q, kn_ref[a:b, cs], _NT) - cum_ref[hh:hh + 1, a:b] * LOG2_E

        s_diag = jnp.where(causal, logits(lo, lo + tr), -jnp.inf)
        m = jnp.max(s_diag, axis=-1, keepdims=True)
        if lo:
            s_past = logits(0, lo)
            m = jnp.maximum(m, jnp.max(s_past, axis=-1, keepdims=True))
        acc = _dot(jnp.exp2(s_diag - m).astype(ACT_DTYPE), va_ref[lo:lo + tr, vs])
        if lo:
            acc = acc + _dot(jnp.exp2(s_past - m).astype(ACT_DTYPE), va_ref[0:lo, vs])
        o_ref[r0:r0 + tr, cs] = (acc[:, :dh] / acc[:, dh:dh + 1]).astype(o_ref.dtype)

    for case in range(nq):
        @pl.when(qi == case)
        def _(case=case):
            for hh in range(nh):
                for r0 in range(0, tq, tr):
                    one_chain(hh, r0, case * tq + r0)


def _fox(p, col0, cum, q_gain, k_gain, layer, batch, seq, heads):
    m = p.shape[0]
    dh = q_gain.shape[-1]
    hp = FOX_HEADS_PER_STEP if heads % FOX_HEADS_PER_STEP == 0 else 1
    hg = heads // hp
    assert col0 % (hp * dh) == 0
    g0 = col0 // (hp * dh)
    tq = _tile(seq, 512)
    nq = seq // tq
    cum = cum.reshape(batch * hg, hp, seq)
    gain = pl.BlockSpec((None, 1, dh), lambda b, g, i: (layer, 0, 0))
    return pl.pallas_call(
        functools.partial(_fox_kernel, tq=tq, nq=nq, dh=dh, scale=dh ** -0.5),
        grid=(batch, hg, nq),
        in_specs=[pl.BlockSpec((tq, hp * dh), lambda b, g, i: (b * nq + i, g0 + g)),
                  pl.BlockSpec((seq, hp * dh), lambda b, g, i: (b, g0 + hg + g)),
                  pl.BlockSpec((seq, hp * dh), lambda b, g, i: (b, g0 + 2 * hg + g)),
                  pl.BlockSpec((None, hp, seq), lambda b, g, i: (b * hg + g, 0, 0)),
                  gain, gain],
        out_specs=pl.BlockSpec((tq, hp * dh), lambda b, g, i: (b * nq + i, g)),
        out_shape=jax.ShapeDtypeStruct((m, heads * dh), ACT_DTYPE),
        scratch_shapes=[pltpu.VMEM((seq, hp * dh), ACT_DTYPE), pltpu.VMEM((seq, 2 * hp * dh), ACT_DTYPE)],
        compiler_params=pltpu.CompilerParams(
            dimension_semantics=("arbitrary", "arbitrary", "arbitrary"),
            vmem_limit_bytes=_vmem_limit(3 * _nbytes((seq, hp * dh), p.dtype),
                                         3 * _nbytes((seq, hp * dh), ACT_DTYPE) + 4 * hp * _nbytes((tq, seq), jnp.float32))),
        name="forgetting_attention",
    )(p, p, p, cum, q_gain, k_gain)


LRU_PAD_ROWS = 8
LRU_CHUNK_ROWS = 32


def _lru_kernel(gate_ref, x_ref, cw_ref, prm_ref, wa_ref, wx_ref, o_ref, xs_ref, us_ref, as_ref, *, conv_width):
    f32 = jnp.float32
    seq, width = x_ref.shape
    pad = LRU_PAD_ROWS
    assert conv_width - 1 <= pad
    body = pl.ds(pad, seq)

    def shifted(ref, s):
        return ref[pl.ds(pad - s, seq), :]

    x = x_ref[...].astype(f32)
    xs_ref[0:pad, :] = jnp.zeros((pad, width), f32)
    xs_ref[body, :] = x
    conv_b, b_a, b_x, lam = (prm_ref[i:i + 1, :] for i in range(4))
    xc = conv_b + x * cw_ref[conv_width - 1:conv_width, :]
    for s in range(1, conv_width):
        xc = xc + shifted(xs_ref, s) * cw_ref[conv_width - 1 - s:conv_width - s, :]
    xcb = xc.astype(ACT_DTYPE)
    r = _sigmoid(_dot(xcb, wa_ref[...]) + b_a)
    i = _sigmoid(_dot(xcb, wx_ref[...]) + b_x)
    log_a = -LRU_C * r * _softplus(-lam)
    a = jnp.exp(log_a)
    w = -jnp.tanh(log_a) * (a * a + 1.0)
    u = jnp.where(w > 0.0, w * lax.rsqrt(w), 0.0) * (i * xc)
    chunk = min(LRU_CHUNK_ROWS, seq)
    h_chunks, carry = [], None
    for c0 in range(0, seq, chunk):
        uc, ac = u[c0:c0 + chunk], a[c0:c0 + chunk]
        base = (c0 // chunk) * (pad + chunk)
        rows = pl.ds(base + pad, chunk)
        us_ref[base:base + pad, :] = jnp.zeros((pad, width), f32)
        as_ref[base:base + pad, :] = jnp.ones((pad, width), f32)
        s = 1
        while s < min(pad, chunk):
            us_ref[rows, :] = uc
            as_ref[rows, :] = ac
            uc = ac * us_ref[pl.ds(base + pad - s, chunk), :] + uc
            ac = ac * as_ref[pl.ds(base + pad - s, chunk), :]
            s *= 2
        while s < chunk:
            uc = jnp.concatenate([uc[:s], ac[s:] * uc[:-s] + uc[s:]], axis=0)
            ac = jnp.concatenate([ac[:s], ac[s:] * ac[:-s]], axis=0)
            s *= 2
        if carry is not None:
            uc = ac * carry + uc
        carry = uc[chunk - 1:chunk]
        h_chunks.append(uc)
    h = jnp.concatenate(h_chunks, axis=0)
    o_ref[...] = (jax.nn.gelu(gate_ref[...].astype(f32)) * h).astype(o_ref.dtype)


def _lru(p_lru, conv_w, prm, w_a, w_x, layer, batch, seq):
    m = p_lru.shape[0]
    nblk, bw = w_a.shape[1], w_a.shape[2]
    conv_width = conv_w.shape[1]
    chunk = min(LRU_CHUNK_ROWS, seq)
    wmat = pl.BlockSpec((None, None, bw, bw), lambda b, c: (layer, c, 0, 0))
    return pl.pallas_call(
        functools.partial(_lru_kernel, conv_width=conv_width),
        grid=(batch, nblk),
        in_specs=[pl.BlockSpec((seq, bw), lambda b, c: (b, c)),
                  pl.BlockSpec((seq, bw), lambda b, c: (b, nblk + c)),
                  pl.BlockSpec((None, conv_width, bw), lambda b, c: (layer, 0, c)),
                  pl.BlockSpec((None, 8, bw), lambda b, c: (layer, 0, c)),
                  wmat, wmat],
        out_specs=pl.BlockSpec((seq, bw), lambda b, c: (b, c)),
        out_shape=jax.ShapeDtypeStruct((m, nblk * bw), ACT_DTYPE),
        scratch_shapes=[pltpu.VMEM((LRU_PAD_ROWS + seq, bw), jnp.float32)]
        + [pltpu.VMEM((seq // chunk * (LRU_PAD_ROWS + chunk), bw), jnp.float32)] * 2,
        compiler_params=pltpu.CompilerParams(
            dimension_semantics=("arbitrary", "arbitrary"),
            vmem_limit_bytes=_vmem_limit(3 * _nbytes((seq, bw), p_lru.dtype),
                                         16 * _nbytes((seq, bw), jnp.float32))),
        name="rg_lru",
    )(p_lru, p_lru, conv_w, prm, w_a, w_x)


def kernel(x, ffn1_norm, ffn1_w_gate, ffn1_w_up, ffn1_w_down, mix_norm, w_in, ret_norm, fox_q_norm, fox_k_norm, fox_f_bias, lru_conv_w, lru_conv_b, lru_w_a, lru_b_a, lru_w_x, lru_b_x, lru_lambda, w_branch_ret, w_branch_fox, w_branch_lru, w_out, ffn2_norm, ffn2_w_gate, ffn2_w_up, ffn2_w_down):
    batch, seq, d = x.shape
    depth = w_in.shape[0]
    ret_heads, ret_dh = ret_norm.shape[1], ret_norm.shape[2]
    fox_heads, fox_dh = fox_f_bias.shape[1], fox_q_norm.shape[1]
    ret_w, fox_w, lru_w = ret_heads * ret_dh, fox_heads * fox_dh, lru_lambda.shape[1]
    assert fox_heads <= V7X_LANES
    c_fox = 4 * ret_w
    c_fg = c_fox + 3 * fox_w
    c_tail = c_fg + fox_heads
    assert w_in.shape[-1] == c_tail + 2 * lru_w + N_BRANCH * d

    f32 = jnp.float32
    as_rows = lambda p: p.reshape(depth, 1, -1)
    ffn1_g, mix_g, ffn2_g = as_rows(ffn1_norm), as_rows(mix_norm), as_rows(ffn2_norm)
    fox_qg, fox_kg = as_rows(fox_q_norm), as_rows(fox_k_norm)
    ret_g = ret_norm.reshape(depth * ret_heads, 1, ret_dh)
    fg_bias = as_rows(jnp.pad(fox_f_bias, ((0, 0), (0, V7X_LANES - fox_heads))))
    zeros = jnp.zeros_like(lru_lambda)
    lru_prm = jnp.stack([lru_conv_b, lru_b_a, lru_b_x, lru_lambda, zeros, zeros, zeros, zeros], axis=1)

    w_in_t = jnp.swapaxes(w_in, 1, 2)
    assert c_fg % 8 == 0 and c_tail % 8 == 0 and c_fg + V7X_LANES <= w_in.shape[-1]

    half = ret_dh // 2
    inv_freq = ROPE_BASE ** (-jnp.arange(half, dtype=f32) / half)
    ang = jnp.arange(seq).astype(f32)[:, None] * inv_freq[None, :]
    cos, sin = jnp.cos(ang), jnp.sin(ang)

    x = x.reshape(batch * seq, d)
    xg, r = _prep(x, ffn1_g, 0)
    for l in range(depth):
        h = _gateup(xg, r, ffn1_w_gate, ffn1_w_up, l)
        x, xg, r = _mm_resid(h, ffn1_w_down, x, l, HALF_STEP, "ffn_down", next_gain=(mix_g, l))

        proj = functools.partial(_mm_nt, xg, r, w_in_t, layer=l)
        p_head = proj(row0=0, n=c_fg, out_dtype=ACT_DTYPE, name="proj_ret_fox")
        p_tail = proj(row0=c_tail, n=2 * lru_w + N_BRANCH * d, out_dtype=ACT_DTYPE, name="proj_lru_gates")
        fg = proj(row0=c_fg, n=V7X_LANES, out_dtype=f32, name="proj_fgate")

        ret = _retention(p_head, cos, sin, ret_g, l, batch, seq, ret_heads)
        cum = _logf_cumsum(fg, fg_bias, l, batch, seq)[:, :fox_heads].reshape(batch * fox_heads, seq)
        fox = _fox(p_head, c_fox, cum, fox_qg, fox_kg, l, batch, seq, fox_heads)
        lru = _lru(p_tail, lru_conv_w, lru_prm, lru_w_a, lru_w_x, l, batch, seq)

        mixed = _merge(ret, fox, lru, w_branch_ret, w_branch_fox, w_branch_lru, p_tail, 2 * lru_w, l)
        x, xg, r = _mm_resid(mixed, w_out, x, l, 1.0, "out_proj", next_gain=(ffn2_g, l))

        h = _gateup(xg, r, ffn2_w_gate, ffn2_w_up, l)
        if l + 1 < depth:
            x, xg, r = _mm_resid(h, ffn2_w_down, x, l, HALF_STEP, "ffn_down", next_gain=(ffn1_g, l + 1))
        else:
            x = _mm_resid(h, ffn2_w_down, x, l, HALF_STEP, "ffn_down")
    return x.reshape(batch, seq, d)
```
